```python
import math
import jax, jax.numpy as jnp
from jax import lax
import numpy as np

D_MODEL = 1024
BATCH = 2
SEQ = 8192
DEPTH = 1

ATT_HEADS = 16
ATT_HEAD_DIM = 64
ATT_WIDTH = ATT_HEADS * ATT_HEAD_DIM
Q_BLOCK = 128
SSM_EXPAND = 2
SSM_INNER = SSM_EXPAND * D_MODEL
SSM_HEAD_DIM = 64
SSM_HEADS = SSM_INNER // SSM_HEAD_DIM
SSM_GROUPS = 4
SSM_HEADS_PER_GROUP = SSM_HEADS // SSM_GROUPS
SSM_STATE = 128
SSM_CONV = 4
SSM_CHUNK = 128
SSM_CONV_DIM = SSM_INNER + 2 * SSM_GROUPS * SSM_STATE
N_BRANCHES = 2
FFN_HIDDEN = -(-8 * D_MODEL // (3 * 256)) * 256
DEEPNORM_ALPHA = (2 * DEPTH) ** 0.25
DEEPNORM_BETA = (8 * DEPTH) ** -0.25
LN_EPS = 1e-5
RMS_EPS = 1e-5
IN_SIZES = (ATT_WIDTH, ATT_WIDTH, ATT_WIDTH, ATT_HEADS, SSM_INNER, SSM_CONV_DIM, SSM_HEADS, N_BRANCHES * D_MODEL)
IN_WIDTH = sum(IN_SIZES)

kernel_name = "fox_ssd_gated_hybrid_deepnorm"


def layer_norm(x, g, b):
    xf = x.astype(jnp.float32)
    mu = jnp.mean(xf, axis=-1, keepdims=True)
    var = jnp.mean(jnp.square(xf - mu), axis=-1, keepdims=True)
    return ((xf - mu) * lax.rsqrt(var + LN_EPS) * g.astype(jnp.float32) + b.astype(jnp.float32)).astype(x.dtype)


def forgetting_attention(q, k, v, log_f):
    bsz, seq, heads, dh = q.shape
    n_blk = seq // Q_BLOCK
    scale = 1.0 / math.sqrt(dh)
    cum = jnp.cumsum(log_f, axis=1).transpose(0, 2, 1)
    kh = k.transpose(0, 2, 1, 3)
    vh = v.transpose(0, 2, 1, 3)
    q_blocks = q.transpose(0, 2, 1, 3).reshape(bsz, heads, n_blk, Q_BLOCK, dh).transpose(2, 0, 1, 3, 4)
    dq_blocks = cum.reshape(bsz, heads, n_blk, Q_BLOCK).transpose(2, 0, 1, 3)
    key_pos = jnp.arange(seq)

    def one_block(args):
        qb, dqb, i = args
        s = jnp.einsum('bhqd,bhkd->bhqk', qb, kh, preferred_element_type=jnp.float32) * scale
        s = s + (dqb[..., :, None] - cum[..., None, :])
        q_pos = i * Q_BLOCK + jnp.arange(Q_BLOCK)
        causal = key_pos[None, :] <= q_pos[:, None]
        p = jax.nn.softmax(jnp.where(causal, s, -jnp.inf), axis=-1).astype(vh.dtype)
        return jnp.einsum('bhqk,bhkd->bhqd', p, vh)

    out = lax.map(one_block, (q_blocks, dq_blocks, jnp.arange(n_blk)))
    return out.transpose(1, 0, 3, 2, 4).reshape(bsz, seq, heads * dh)


def causal_depthwise_conv(u, w, b):
    out = lax.conv_general_dilated(u, w[:, None, :], window_strides=(1,), padding=[(SSM_CONV - 1, 0)],
                                   dimension_numbers=('NWC', 'WIO', 'NWC'), feature_group_count=u.shape[-1])
    return out + b


def ssd_chunked(x, dt, a, bmat, cmat):
    bsz, seq, heads, hd = x.shape
    nc, L = seq // SSM_CHUNK, SSM_CHUNK
    G, R, N = SSM_GROUPS, SSM_HEADS_PER_GROUP, SSM_STATE
    xc = x.reshape(bsz, nc, L, G, R, hd)
    dtc = dt.reshape(bsz, nc, L, G, R)
    bc = bmat.reshape(bsz, nc, L, G, N)
    cc = cmat.reshape(bsz, nc, L, G, N)
    da = dtc * a.reshape(G, R)
    acum = jnp.cumsum(da, axis=2).transpose(0, 1, 3, 4, 2)
    xdt = xc * dtc[..., None]
    idx = jnp.arange(L)
    causal = idx[:, None] >= idx[None, :]
    decay = jnp.exp(jnp.where(causal, acum[..., :, None] - acum[..., None, :], -jnp.inf))
    cb = jnp.einsum('bclgn,bcsgn->bcgls', cc, bc, preferred_element_type=jnp.float32)
    y_diag = jnp.einsum('bcgls,bcgrls,bcsgrp->bclgrp', cb, decay, xdt)
    decay_to_end = jnp.exp(acum[..., -1:] - acum)
    states = jnp.einsum('bcsgn,bcgrs,bcsgrp->bcgrpn', bc, decay_to_end, xdt)
    chunk_decay = jnp.exp(acum[..., -1])

    def step(h, inp):
        st, dec = inp
        return h * dec[..., None, None] + st, h

    h0 = jnp.zeros((bsz, G, R, hd, N), jnp.float32)
    _, h_prev = lax.scan(step, h0, (states.transpose(1, 0, 2, 3, 4, 5), chunk_decay.transpose(1, 0, 2, 3)))
    h_prev = h_prev.transpose(1, 0, 2, 3, 4, 5)
    y_off = jnp.einsum('bclgn,bcgrpn,bcgrl->bclgrp', cc, h_prev, jnp.exp(acum))
    return (y_diag + y_off).reshape(bsz, seq, heads, hd)


def mamba2_branch(z, xbc, dt_raw, conv_w, conv_b, dt_bias, a_log, d_skip, norm_w):
    bsz, seq, _ = z.shape
    xbc = jax.nn.silu(causal_depthwise_conv(xbc, conv_w, conv_b))
    xs, bm, cm = jnp.split(xbc, [SSM_INNER, SSM_INNER + SSM_GROUPS * SSM_STATE], axis=-1)
    xs = xs.reshape(bsz, seq, SSM_HEADS, SSM_HEAD_DIM)
    bm = bm.reshape(bsz, seq, SSM_GROUPS, SSM_STATE)
    cm = cm.reshape(bsz, seq, SSM_GROUPS, SSM_STATE)
    dt = jax.nn.softplus(dt_raw.astype(jnp.float32) + dt_bias.astype(jnp.float32))
    a = -jnp.exp(a_log.astype(jnp.float32))
    y = ssd_chunked(xs, dt, a, bm, cm) + d_skip.astype(jnp.float32)[:, None] * xs
    u = (y.reshape(bsz, seq, SSM_INNER) * jax.nn.silu(z.astype(jnp.float32))).reshape(bsz, seq, SSM_GROUPS, -1)
    u = u * lax.rsqrt(jnp.mean(jnp.square(u), axis=-1, keepdims=True) + RMS_EPS)
    return (u.reshape(bsz, seq, SSM_INNER) * norm_w.astype(jnp.float32)).astype(z.dtype)


def setup_inputs(seed: int = 0) -> dict:
    key = jax.random.key(seed)
    ks = jax.random.split(key, 20)
    f32 = jnp.float32

    def nrm(k, shape, fan_in, mult=1.0):
        return jax.random.normal(k, shape, f32) * (fan_in ** -0.5) * mult

    dt0 = jnp.exp(jax.random.uniform(ks[5], (DEPTH, SSM_HEADS), f32, math.log(1e-3), math.log(1e-1)))
    return {
        "x": jax.random.normal(ks[0], (BATCH, SEQ, D_MODEL), f32),
        "w_in": nrm(ks[1], (DEPTH, D_MODEL, IN_WIDTH), D_MODEL),
        "b_forget": jax.random.uniform(ks[2], (DEPTH, ATT_HEADS), f32, 1.0, 6.0),
        "conv_w": jax.random.uniform(ks[3], (DEPTH, SSM_CONV, SSM_CONV_DIM), f32, -0.5, 0.5),
        "conv_b": 0.02 * jax.random.normal(ks[4], (DEPTH, SSM_CONV_DIM), f32),
        "dt_bias": dt0 + jnp.log(-jnp.expm1(-dt0)),
        "a_log": jnp.log(jax.random.uniform(ks[6], (DEPTH, SSM_HEADS), f32, 1.0, 16.0)),
        "d_skip": 1.0 + 0.1 * jax.random.normal(ks[7], (DEPTH, SSM_HEADS), f32),
        "ssm_norm_w": 1.0 + 0.1 * jax.random.normal(ks[8], (DEPTH, SSM_INNER), f32),
        "w_proj_attn": nrm(ks[9], (DEPTH, ATT_WIDTH, D_MODEL), ATT_WIDTH, DEEPNORM_BETA),
        "w_proj_ssm": nrm(ks[10], (DEPTH, SSM_INNER, D_MODEL), SSM_INNER, DEEPNORM_BETA),
        "b_gates": 0.1 * jax.random.normal(ks[11], (DEPTH, N_BRANCHES * D_MODEL), f32),
        "w_out": nrm(ks[12], (DEPTH, D_MODEL, D_MODEL), D_MODEL, DEEPNORM_BETA),
        "ln1_g": 1.0 + 0.1 * jax.random.normal(ks[13], (DEPTH, D_MODEL), f32),
        "ln1_b": 0.02 * jax.random.normal(ks[14], (DEPTH, D_MODEL), f32),
        "w_ffn_gate": nrm(ks[15], (DEPTH, D_MODEL, FFN_HIDDEN), D_MODEL),
        "w_ffn_up": nrm(ks[16], (DEPTH, D_MODEL, FFN_HIDDEN), D_MODEL),
        "w_ffn_down": nrm(ks[17], (DEPTH, FFN_HIDDEN, D_MODEL), FFN_HIDDEN, DEEPNORM_BETA),
        "ln2_g": 1.0 + 0.1 * jax.random.normal(ks[18], (DEPTH, D_MODEL), f32),
        "ln2_b": 0.02 * jax.random.normal(ks[19], (DEPTH, D_MODEL), f32),
    }


def reference(x, w_in, b_forget, conv_w, conv_b, dt_bias, a_log, d_skip, ssm_norm_w, w_proj_attn,
              w_proj_ssm, b_gates, w_out, ln1_g, ln1_b, w_ffn_gate, w_ffn_up, w_ffn_down, ln2_g, ln2_b):
    bsz, seq, _ = x.shape
    split_idx = [int(i) for i in np.cumsum(IN_SIZES)[:-1]]
    for l in range(DEPTH):
        proj = x @ w_in[l]
        q, k, v, f_logit, z, xbc, dt_raw, gate_logit = jnp.split(proj, split_idx, axis=-1)
        log_f = jax.nn.log_sigmoid(f_logit.astype(jnp.float32) + b_forget[l].astype(jnp.float32))
        hs = (bsz, seq, ATT_HEADS, ATT_HEAD_DIM)
        attn = forgetting_attention(q.reshape(hs), k.reshape(hs), v.reshape(hs), log_f)
        attn_d = attn @ w_proj_attn[l]
        ssm = mamba2_branch(z, xbc, dt_raw, conv_w[l], conv_b[l], dt_bias[l], a_log[l], d_skip[l], ssm_norm_w[l])
        ssm_d = ssm @ w_proj_ssm[l]
        gates = jax.nn.sigmoid(gate_logit + b_gates[l]).reshape(bsz, seq, N_BRANCHES, D_MODEL)
        mixed = (gates[:, :, 0] * attn_d + gates[:, :, 1] * ssm_d) @ w_out[l]
        x = layer_norm(DEEPNORM_ALPHA * x + mixed, ln1_g[l], ln1_b[l])
        h = (jax.nn.silu(x @ w_ffn_gate[l]) * (x @ w_ffn_up[l])) @ w_ffn_down[l]
        x = layer_norm(DEEPNORM_ALPHA * x + h, ln2_g[l], ln2_b[l])
    return x
```

```python
import functools
import math

import numpy as np
import jax
import jax.numpy as jnp
from jax import lax
from jax.experimental import pallas as pl
from jax.experimental.pallas import tpu as pltpu

F32 = jnp.float32
BF16 = jnp.bfloat16

D_MODEL = 1024
ATT_HEADS = 16
ATT_HEAD_DIM = 64
ATT_WIDTH = ATT_HEADS * ATT_HEAD_DIM
SSM_INNER = 2048
SSM_HEAD_DIM = 64
SSM_HEADS = 32
SSM_GROUPS = 4
SSM_STATE = 128
SSM_CONV = 4
SSM_CHUNK = 128
SSM_CONV_DIM = SSM_INNER + 2 * SSM_GROUPS * SSM_STATE
GROUP_WIDTH = SSM_INNER // SSM_GROUPS
FFN_HIDDEN = 2816
DEPTH = 1
DEEPNORM_ALPHA = (2 * DEPTH) ** 0.25
LN_EPS = 1e-5
RMS_EPS = 1e-5
IN_SIZES = (ATT_WIDTH, ATT_WIDTH, ATT_WIDTH, ATT_HEADS, SSM_INNER, SSM_CONV_DIM, SSM_HEADS, 2 * D_MODEL)
IN_OFFS = tuple(int(v) for v in np.concatenate([[0], np.cumsum(IN_SIZES)]))

LANES = 128
VMEM_LIMIT = 56 * 1024 * 1024

TM_INPROJ = 256
ATT_BLOCK = 512
TM_MIX = 512
TM_FFN = 512
SMALL_W = 128
SMALL_T = 64
DT_OFF = 0
F_OFF = 32
XQ_ROWS = 16
NEG_BIG = -1e30


def _sigmoid(v):
    return 1.0 / (1.0 + jnp.exp(-v))


def _softplus(v):
    return jnp.maximum(v, 0.0) + jnp.log1p(jnp.exp(-jnp.abs(v)))


def _split2(v):
    hi = v.astype(BF16)
    lo = (v - hi.astype(F32)).astype(BF16)
    return hi, lo


def _split3(v):
    hi = v.astype(BF16)
    r = v - hi.astype(F32)
    lo = r.astype(BF16)
    lolo = (r - lo.astype(F32)).astype(BF16)
    return hi, lo, lolo


def _scan_rows(v):
    n = v.shape[0]
    row = lax.broadcasted_iota(jnp.int32, v.shape, 0)
    d = 1
    while d < n:
        v = v + jnp.where(row >= d, pltpu.roll(v, d, 0), 0.0)
        d *= 2
    return v


def _scan_lanes(v):
    n = v.shape[1]
    col = lax.broadcasted_iota(jnp.int32, v.shape, 1)
    d = 1
    while d < n:
        v = v + jnp.where(col >= d, pltpu.roll(v, d, 1), 0.0)
        d *= 2
    return v


def _layer_norm(y, g, b):
    mu = jnp.mean(y, axis=-1, keepdims=True)
    yc = y - mu
    var = jnp.mean(yc * yc, axis=-1, keepdims=True)
    return yc * lax.rsqrt(var + LN_EPS) * g + b


def _const_spec(shape):
    nd = len(shape)
    return pl.BlockSpec(shape, lambda *_: (0,) * nd, pipeline_mode=pl.Buffered(1))


def _params(*sem):
    return pltpu.CompilerParams(dimension_semantics=sem, vmem_limit_bytes=VMEM_LIMIT)


WA_K = (0, 1024)
WA_Z = (1024, 3072)
WA_XBC = (3072, 6144)
WA_G = (6144, 8192)
WA_S = (8192, 8192 + SMALL_W)
WB_Q = (0, 1024)
WB_V = (1024, 2048)
WB_S = (2048, 2048 + SMALL_T)


def _inproj_kernel(x_ref, wa_ref, wbt_ref, k_ref, z_ref, xbc_ref, gl_ref, sm_ref, qt_ref, vt_ref, smt_ref):
    xb = x_ref[...].astype(BF16)

    def seg(r):
        return jnp.dot(xb, wa_ref[:, r[0]:r[1]], preferred_element_type=F32)

    def segt(r):
        return lax.dot_general(wbt_ref[r[0]:r[1], :], xb, (((1,), (1,)), ((), ())),
                               preferred_element_type=F32)

    k_ref[...] = seg(WA_K).astype(BF16)
    z_ref[...] = seg(WA_Z)
    xbc_ref[...] = seg(WA_XBC)
    gl_ref[...] = seg(WA_G)
    sm_ref[...] = seg(WA_S)
    qt_ref[...] = segt(WB_Q).astype(BF16)
    vt_ref[...] = segt(WB_V).astype(BF16)
    smt_ref[...] = segt(WB_S)


def _inproj(x2, wa, wbt):
    t = x2.shape[0]
    tm = TM_INPROJ
    tok = lambda w: pl.BlockSpec((tm, w), lambda i: (i, 0))
    feat = lambda h: pl.BlockSpec((h, tm), lambda i: (0, i))
    return pl.pallas_call(
        _inproj_kernel,
        grid=(t // tm,),
        in_specs=[tok(D_MODEL), _const_spec(wa.shape), _const_spec(wbt.shape)],
        out_specs=[tok(1024), tok(2048), tok(3072), tok(2048), tok(SMALL_W),
                   feat(1024), feat(1024), feat(SMALL_T)],
        out_shape=[
            jax.ShapeDtypeStruct((t, 1024), BF16),
            jax.ShapeDtypeStruct((t, 2048), F32),
            jax.ShapeDtypeStruct((t, 3072), F32),
            jax.ShapeDtypeStruct((t, 2048), F32),
            jax.ShapeDtypeStruct((t, SMALL_W), F32),
            jax.ShapeDtypeStruct((1024, t), BF16),
            jax.ShapeDtypeStruct((1024, t), BF16),
            jax.ShapeDtypeStruct((SMALL_T, t), F32),
        ],
        compiler_params=_params("arbitrary"),
        name="inproj",
    )(x2, wa, wbt)


def _attnprep_kernel(sm_ref, smt_ref, k_ref, bfrow_ref, bfcol_ref, shi_ref, slo_ref, sll_ref, crow_ref,
                     phi_ref, plo_ref, pll_ref, ka_ref, xq_ref, base_ref, carry_ref):
    i = pl.program_id(1)

    @pl.when(i == 0)
    def _():
        carry_ref[...] = jnp.zeros_like(carry_ref)

    g = sm_ref.shape[0]
    v = sm_ref[...] + bfrow_ref[...]
    lf = -_softplus(-v)
    rk = _scan_rows(lf)
    base_ref[0] = carry_ref[...]
    carry_ref[...] = carry_ref[...] + rk[g - 1:g, :]
    hi, lo, ll = _split3(rk)
    extras = (jnp.dot(hi, shi_ref[...], preferred_element_type=F32)
              + jnp.dot(lo, slo_ref[...], preferred_element_type=F32)
              + jnp.dot(ll, sll_ref[...], preferred_element_type=F32)
              + crow_ref[...])
    lane = lax.broadcasted_iota(jnp.int32, (g, LANES), 1)
    low = lane < ATT_HEAD_DIM
    for j in range(ATT_HEADS // 2):
        kk = k_ref[:, LANES * j:LANES * (j + 1)].astype(F32)
        ev = extras[:, 2 * LANES * j:2 * LANES * j + LANES]
        od = extras[:, 2 * LANES * j + LANES:2 * LANES * (j + 1)]
        ka_ref[:, 2 * LANES * j:2 * LANES * j + LANES] = jnp.where(low, kk, ev).astype(BF16)
        ka_ref[:, 2 * LANES * j + LANES:2 * LANES * (j + 1)] = jnp.where(low, od, kk).astype(BF16)

    vt = smt_ref[...] + bfcol_ref[...]
    lft = -_softplus(-vt)
    rq = _scan_lanes(lft)
    hq, lq, llq = _split3(rq)
    nrow = ATT_HEADS * XQ_ROWS
    row = lax.broadcasted_iota(jnp.int32, (nrow, g), 0)
    ones = jnp.where((row & (XQ_ROWS - 1)) < 3, 1.0, 0.0)
    xq = (jnp.dot(phi_ref[...], hq, preferred_element_type=F32)
          + jnp.dot(plo_ref[...], lq, preferred_element_type=F32)
          + jnp.dot(pll_ref[...], llq, preferred_element_type=F32)
          + ones)
    xq_ref[...] = xq.astype(BF16)


def _attnprep_consts():
    s_hi = np.zeros((SMALL_W, ATT_HEADS * LANES), np.float32)
    s_lo = np.zeros_like(s_hi)
    s_ll = np.zeros_like(s_hi)
    crow = np.zeros((1, ATT_HEADS * LANES), np.float32)
    p_hi = np.zeros((ATT_HEADS * XQ_ROWS, SMALL_T), np.float32)
    p_lo = np.zeros_like(p_hi)
    p_ll = np.zeros_like(p_hi)
    for h in range(ATT_HEADS):
        off = LANES * h + (ATT_HEAD_DIM if h % 2 == 0 else 0)
        s_hi[F_OFF + h, off + 0] = -1.0
        s_lo[F_OFF + h, off + 1] = -1.0
        s_ll[F_OFF + h, off + 2] = -1.0
        crow[0, off + 3:off + 6] = 1.0
        p_hi[XQ_ROWS * h + 3, F_OFF + h] = 1.0
        p_lo[XQ_ROWS * h + 4, F_OFF + h] = 1.0
        p_ll[XQ_ROWS * h + 5, F_OFF + h] = 1.0
    b = lambda a: jnp.asarray(a, BF16)
    return b(s_hi), b(s_lo), b(s_ll), jnp.asarray(crow), b(p_hi), b(p_lo), b(p_ll)


def _attnprep(sm, smt, k, b_forget, bsz, seq):
    g = ATT_BLOCK
    nb = seq // g
    s_hi, s_lo, s_ll, crow, p_hi, p_lo, p_ll = _attnprep_consts()
    bfrow = jnp.zeros((1, SMALL_W), F32).at[0, F_OFF:F_OFF + ATT_HEADS].set(b_forget.astype(F32))
    bfcol = jnp.zeros((SMALL_T, 1), F32).at[F_OFF:F_OFF + ATT_HEADS, 0].set(b_forget.astype(F32))
    bfcol = jnp.broadcast_to(bfcol, (SMALL_T, g))
    t = bsz * seq
    ka, xq, base = pl.pallas_call(
        _attnprep_kernel,
        grid=(bsz, nb),
        in_specs=[
            pl.BlockSpec((g, SMALL_W), lambda b, i: (b * nb + i, 0)),
            pl.BlockSpec((SMALL_T, g), lambda b, i: (0, b * nb + i)),
            pl.BlockSpec((g, ATT_WIDTH), lambda b, i: (b * nb + i, 0)),
            _const_spec(bfrow.shape), _const_spec(bfcol.shape),
            _const_spec(s_hi.shape), _const_spec(s_lo.shape), _const_spec(s_ll.shape), _const_spec(crow.shape),
            _const_spec(p_hi.shape), _const_spec(p_lo.shape), _const_spec(p_ll.shape),
        ],
        out_specs=[
            pl.BlockSpec((g, ATT_HEADS * LANES), lambda b, i: (b * nb + i, 0)),
            pl.BlockSpec((ATT_HEADS * XQ_ROWS, g), lambda b, i: (0, b * nb + i)),
            pl.BlockSpec((1, 1, SMALL_W), lambda b, i: (b * nb + i, 0, 0)),
        ],
        out_shape=[
            jax.ShapeDtypeStruct((t, ATT_HEADS * LANES), BF16),
            jax.ShapeDtypeStruct((ATT_HEADS * XQ_ROWS, t), BF16),
            jax.ShapeDtypeStruct((bsz * nb, 1, SMALL_W), F32),
        ],
        scratch_shapes=[pltpu.VMEM((1, SMALL_W), F32)],
        compiler_params=_params("arbitrary", "arbitrary"),
        name="attnprep",
    )(sm, smt, k, bfrow, bfcol, s_hi, s_lo, s_ll, crow, p_hi, p_lo, p_ll)
    base = base.reshape(bsz, nb, SMALL_W)[:, :, F_OFF:F_OFF + ATT_HEADS]
    base = jnp.transpose(base, (0, 2, 1)).reshape(-1)
    return ka, xq, base


def _attn_kernel(base_ref, qt_ref, xq_ref, ka_ref, vt_ref, o_ref, *, nb):
    b = pl.program_id(0)
    pr = pl.program_id(1)
    i = pl.program_id(2)
    tq = qt_ref.shape[1]
    tk = tq
    dh = ATT_HEAD_DIM
    zpad = jnp.zeros((LANES - dh - XQ_ROWS, tq), BF16)
    qa = (jnp.concatenate([qt_ref[0:dh, :], xq_ref[0:XQ_ROWS, :], zpad], axis=0),
          jnp.concatenate([xq_ref[XQ_ROWS:2 * XQ_ROWS, :], zpad, qt_ref[dh:2 * dh, :]], axis=0))
    boff = [(b * ATT_HEADS + 2 * pr + e) * nb for e in range(2)]
    row = lax.broadcasted_iota(jnp.int32, (tk, tq), 0)
    col = lax.broadcasted_iota(jnp.int32, (tk, tq), 1)
    causal = row <= col

    def step(j, carry, masked):
        off = pl.multiple_of(j * tk, tk)
        out = []
        for e in range(2):
            m, l, acc = carry[e]
            kj = ka_ref[pl.ds(off, tk), LANES * e:LANES * (e + 1)]
            s = jnp.dot(kj, qa[e], preferred_element_type=F32)
            if masked:
                s = jnp.where(causal, s, NEG_BIG)
            c = base_ref[boff[e] + i] - base_ref[boff[e] + j]
            mt = jnp.max(s, axis=0, keepdims=True)
            m_new = jnp.maximum(m, mt + c)
            p = jnp.exp(s - (m_new - c))
            alpha = jnp.exp(m - m_new)
            l = alpha * l + jnp.sum(p, axis=0, keepdims=True)
            vj = vt_ref[dh * e:dh * (e + 1), pl.ds(off, tk)]
            acc = alpha * acc + jnp.dot(vj, p.astype(BF16), preferred_element_type=F32)
            out.append((m_new, l, acc))
        return tuple(out)

    init = tuple((jnp.full((1, tq), NEG_BIG, F32), jnp.zeros((1, tq), F32), jnp.zeros((dh, tq), F32))
                 for _ in range(2))
    carry = lax.fori_loop(0, i, lambda j, c: step(j, c, False), init)
    carry = step(i, carry, True)
    ot = jnp.concatenate([carry[e][2] / carry[e][1] for e in range(2)], axis=0)
    o_ref[...] = ot.T.astype(o_ref.dtype)


def _attention(base, qt, xq, ka, vt, bsz, seq):
    tq = ATT_BLOCK
    nb = seq // tq
    npair = ATT_HEADS // 2
    t = bsz * seq
    return pl.pallas_call(
        functools.partial(_attn_kernel, nb=nb),
        grid=(bsz, npair, nb),
        in_specs=[
            pl.BlockSpec(memory_space=pltpu.SMEM),
            pl.BlockSpec((2 * ATT_HEAD_DIM, tq), lambda b, p, i: (p, b * nb + i)),
            pl.BlockSpec((2 * XQ_ROWS, tq), lambda b, p, i: (p, b * nb + i)),
            pl.BlockSpec((seq, 2 * LANES), lambda b, p, i: (b, p)),
            pl.BlockSpec((2 * ATT_HEAD_DIM, seq), lambda b, p, i: (p, b)),
        ],
        out_specs=pl.BlockSpec((tq, 2 * ATT_HEAD_DIM), lambda b, p, i: (b * nb + i, p)),
        out_shape=jax.ShapeDtypeStruct((t, ATT_WIDTH), BF16),
        compiler_params=_params("arbitrary", "arbitrary", "arbitrary"),
        name="attn",
    )(base, qt, xq, ka, vt)


def _ssd_kernel(z_ref, xbc_ref, sm_ref, smt_ref, cw_ref, cb_ref, dtbrow_ref, dtbcol_ref, arow_ref, acol_ref,
                dexp_ref, nw_ref, e2_ref, o_ref, prev_ref, state_ref):
    c = pl.program_id(1)
    L = SSM_CHUNK
    N = SSM_STATE
    GW = GROUP_WIDTH

    @pl.when(c == 0)
    def _():
        prev_ref[...] = jnp.zeros_like(prev_ref)
        state_ref[...] = jnp.zeros_like(state_ref)

    u = xbc_ref[...]
    pv = prev_ref[...]
    row = lax.broadcasted_iota(jnp.int32, u.shape, 0)
    acc = u * cw_ref[SSM_CONV - 1:SSM_CONV, :] + cb_ref[...]
    for s in range(1, SSM_CONV):
        sh = jnp.where(row >= s, pltpu.roll(u, s, 0), pltpu.roll(pv, s, 0))
        acc = acc + sh * cw_ref[SSM_CONV - 1 - s:SSM_CONV - s, :]
    prev_ref[...] = u
    xc = acc * _sigmoid(acc)
    xs = xc[:, :SSM_INNER]
    bm = xc[:, SSM_INNER:SSM_INNER + SSM_GROUPS * N].astype(BF16)
    cm = xc[:, SSM_INNER + SSM_GROUPS * N:].astype(BF16)

    a_row = -jnp.exp(arow_ref[...])
    dt = _softplus(sm_ref[...] + dtbrow_ref[...])
    acum = _scan_rows(dt * a_row)
    a_col = -jnp.exp(acol_ref[...])
    dtt = _softplus(smt_ref[...] + dtbcol_ref[...])
    acumt = _scan_lanes(dtt * a_col)
    alast = acum[L - 1:L, :]
    ea = jnp.exp(acum)
    w2 = dt * jnp.exp(alast - acum)

    def expand(v):
        hi, lo = _split2(v)
        return jnp.dot(jnp.concatenate([hi, lo], axis=1), e2_ref[...], preferred_element_type=F32)

    dt_e = expand(dt)
    ea_e = expand(ea)
    w2_e = expand(w2)
    cd_e = ea_e[L - 1:L, :]

    xdt = (xs * dt_e).astype(BF16)
    xw2 = (xs * w2_e).astype(BF16)

    r2 = lax.broadcasted_iota(jnp.int32, (L, L), 0)
    c2 = lax.broadcasted_iota(jnp.int32, (L, L), 1)
    tri = r2 >= c2
    lane = lax.broadcasted_iota(jnp.int32, (L, LANES), 1)
    low = lane < SSM_HEAD_DIM

    ys = []
    for g in range(SSM_GROUPS):
        bg = bm[:, N * g:N * (g + 1)]
        cg = cm[:, N * g:N * (g + 1)]
        cb = lax.dot_general(cg, bg, (((1,), (1,)), ((), ())), preferred_element_type=F32)
        st_prev = state_ref[g]
        y_off = jnp.dot(cg, st_prev.astype(BF16), preferred_element_type=F32) * ea_e[:, GW * g:GW * (g + 1)]
        st_c = lax.dot_general(bg, xw2[:, GW * g:GW * (g + 1)], (((0,), (0,)), ((), ())),
                               preferred_element_type=F32)
        state_ref[g] = st_prev * cd_e[:, GW * g:GW * (g + 1)] + st_c
        yd = []
        for pj in range(GW // LANES):
            h0 = (GW // SSM_HEAD_DIM) * g + 2 * pj
            ms = []
            for h in (h0, h0 + 1):
                diff = acum[:, h:h + 1] - acumt[h:h + 1, :]
                ms.append((cb * jnp.exp(jnp.where(tri, diff, NEG_BIG))).astype(BF16))
            xp = xdt[:, LANES * (h0 // 2):LANES * (h0 // 2 + 1)]
            zero = jnp.zeros_like(xp)
            rhs = jnp.concatenate([jnp.where(low, xp, zero), jnp.where(low, zero, xp)], axis=0)
            yd.append(jnp.dot(jnp.concatenate(ms, axis=1), rhs, preferred_element_type=F32))
        ys.append(jnp.concatenate(yd, axis=1) + y_off)
    y = jnp.concatenate(ys, axis=1) + dexp_ref[...] * xs

    zz = z_ref[...]
    uu = y * (zz * _sigmoid(zz))
    outs = []
    for g in range(SSM_GROUPS):
        ug = uu[:, GW * g:GW * (g + 1)]
        ms_ = jnp.mean(ug * ug, axis=-1, keepdims=True)
        outs.append(ug * lax.rsqrt(ms_ + RMS_EPS))
    o_ref[...] = (jnp.concatenate(outs, axis=1) * nw_ref[...]).astype(o_ref.dtype)


def _ssd(z, xbc, sm, smt, conv_w, conv_b, dt_bias, a_log, d_skip, norm_w, bsz, seq):
    L = SSM_CHUNK
    nc = seq // L
    t = bsz * seq
    dtbrow = jnp.zeros((1, SMALL_W), F32).at[0, DT_OFF:DT_OFF + SSM_HEADS].set(dt_bias.astype(F32))
    arow = jnp.zeros((1, SMALL_W), F32).at[0, DT_OFF:DT_OFF + SSM_HEADS].set(a_log.astype(F32))
    dtbcol = jnp.zeros((SMALL_T, 1), F32).at[DT_OFF:DT_OFF + SSM_HEADS, 0].set(dt_bias.astype(F32))
    acol = jnp.zeros((SMALL_T, 1), F32).at[DT_OFF:DT_OFF + SSM_HEADS, 0].set(a_log.astype(F32))
    dtbcol = jnp.broadcast_to(dtbcol, (SMALL_T, L))
    acol = jnp.broadcast_to(acol, (SMALL_T, L))
    dexp = jnp.repeat(d_skip.astype(F32), SSM_HEAD_DIM)[None, :]
    e1 = np.zeros((SMALL_W, SSM_INNER), np.float32)
    for h in range(SSM_HEADS):
        e1[DT_OFF + h, SSM_HEAD_DIM * h:SSM_HEAD_DIM * (h + 1)] = 1.0
    e2 = jnp.asarray(np.concatenate([e1, e1], axis=0), BF16)
    tok = lambda w: pl.BlockSpec((L, w), lambda b, c: (b * nc + c, 0))
    return pl.pallas_call(
        _ssd_kernel,
        grid=(bsz, nc),
        in_specs=[
            tok(SSM_INNER), tok(SSM_CONV_DIM), tok(SMALL_W),
            pl.BlockSpec((SMALL_T, L), lambda b, c: (0, b * nc + c)),
            _const_spec((SSM_CONV, SSM_CONV_DIM)), _const_spec((1, SSM_CONV_DIM)),
            _const_spec(dtbrow.shape), _const_spec(dtbcol.shape), _const_spec(arow.shape), _const_spec(acol.shape),
            _const_spec(dexp.shape), _const_spec((1, SSM_INNER)), _const_spec(e2.shape),
        ],
        out_specs=tok(SSM_INNER),
        out_shape=jax.ShapeDtypeStruct((t, SSM_INNER), BF16),
        scratch_shapes=[pltpu.VMEM((L, SSM_CONV_DIM), F32),
                        pltpu.VMEM((SSM_GROUPS, SSM_STATE, GROUP_WIDTH), F32)],
        compiler_params=_params("arbitrary", "arbitrary"),
        name="ssd",
    )(z, xbc, sm, smt, conv_w.astype(F32), conv_b.astype(F32)[None, :], dtbrow, dtbcol, arow, acol,
      dexp, norm_w.astype(F32)[None, :], e2)


def _mix_kernel(attn_ref, ssm_ref, gl_ref, x_ref, wpa_ref, wps_ref, wout_ref, bg_ref, g_ref, b_ref, o_ref):
    ad = jnp.dot(attn_ref[...], wpa_ref[...], preferred_element_type=F32)
    sd = jnp.dot(ssm_ref[...], wps_ref[...], preferred_element_type=F32)
    gates = _sigmoid(gl_ref[...] + bg_ref[...])
    mix = gates[:, :D_MODEL] * ad + gates[:, D_MODEL:] * sd
    mixed = jnp.dot(mix.astype(BF16), wout_ref[...], preferred_element_type=F32)
    o_ref[...] = _layer_norm(DEEPNORM_ALPHA * x_ref[...] + mixed, g_ref[...], b_ref[...])


def _mix(attn, ssm, gl, x2, wpa, wps, wout, b_gates, ln_g, ln_b):
    t = x2.shape[0]
    tm = TM_MIX
    tok = lambda w: pl.BlockSpec((tm, w), lambda i: (i, 0))
    return pl.pallas_call(
        _mix_kernel,
        grid=(t // tm,),
        in_specs=[tok(ATT_WIDTH), tok(SSM_INNER), tok(2 * D_MODEL), tok(D_MODEL),
                  _const_spec(wpa.shape), _const_spec(wps.shape), _const_spec(wout.shape),
                  _const_spec((1, 2 * D_MODEL)), _const_spec((1, D_MODEL)), _const_spec((1, D_MODEL))],
        out_specs=tok(D_MODEL),
        out_shape=jax.ShapeDtypeStruct((t, D_MODEL), F32),
        compiler_params=_params("arbitrary"),
        name="mix",
    )(attn, ssm, gl, x2, wpa, wps, wout, b_gates.astype(F32)[None, :], ln_g.astype(F32)[None, :],
      ln_b.astype(F32)[None, :])


def _ffn_kernel(x_ref, wg_ref, wu_ref, wd_ref, g_ref, b_ref, o_ref):
    x = x_ref[...]
    xb = x.astype(BF16)
    hg = jnp.dot(xb, wg_ref[...], preferred_element_type=F32)
    hu = jnp.dot(xb, wu_ref[...], preferred_element_type=F32)
    h = (hg * _sigmoid(hg) * hu).astype(BF16)
    d = jnp.dot(h, wd_ref[...], preferred_element_type=F32)
    o_ref[...] = _layer_norm(DEEPNORM_ALPHA * x + d, g_ref[...], b_ref[...])


def _ffn(x1, wg, wu, wd, ln_g, ln_b):
    t = x1.shape[0]
    tm = TM_FFN
    tok = pl.BlockSpec((tm, D_MODEL), lambda i: (i, 0))
    return pl.pallas_call(
        _ffn_kernel,
        grid=(t // tm,),
        in_specs=[tok, _const_spec(wg.shape), _const_spec(wu.shape), _const_spec(wd.shape),
                  _const_spec((1, D_MODEL)), _const_spec((1, D_MODEL))],
        out_specs=tok,
        out_shape=jax.ShapeDtypeStruct((t, D_MODEL), F32),
        compiler_params=_params("arbitrary"),
        name="ffn",
    )(x1, wg, wu, wd, ln_g.astype(F32)[None, :], ln_b.astype(F32)[None, :])


def _pack_in_weights(w):
    o = IN_OFFS
    q, k, v, f = w[:, o[0]:o[1]], w[:, o[1]:o[2]], w[:, o[2]:o[3]], w[:, o[3]:o[4]]
    z, xbc, dt, gate = w[:, o[4]:o[5]], w[:, o[5]:o[6]], w[:, o[6]:o[7]], w[:, o[7]:o[8]]
    small = jnp.zeros((D_MODEL, SMALL_W), w.dtype)
    small = small.at[:, DT_OFF:DT_OFF + SSM_HEADS].set(dt).at[:, F_OFF:F_OFF + ATT_HEADS].set(f)
    wa = jnp.concatenate([k, z, xbc, gate, small], axis=1).astype(BF16)
    scale = 1.0 / math.sqrt(ATT_HEAD_DIM)
    wbt = jnp.concatenate([(q * scale).T, v.T, small[:, :SMALL_T].T], axis=0).astype(BF16)
    return wa, wbt


def kernel(x, w_in, b_forget, conv_w, conv_b, dt_bias, a_log, d_skip, ssm_norm_w, w_proj_attn,
           w_proj_ssm, b_gates, w_out, ln1_g, ln1_b, w_ffn_gate, w_ffn_up, w_ffn_down, ln2_g, ln2_b):
    bsz, seq, dm = x.shape
    assert dm == D_MODEL and seq % ATT_BLOCK == 0 and seq % SSM_CHUNK == 0
    assert w_in.shape[0] == DEPTH
    x2 = x.reshape(bsz * seq, dm)
    for l in range(DEPTH):
        wa, wbt = _pack_in_weights(w_in[l])
        k, z, xbc, gl, sm, qt, vt, smt = _inproj(x2, wa, wbt)
        ka, xq, base = _attnprep(sm, smt, k, b_forget[l], bsz, seq)
        attn = _attention(base, qt, xq, ka, vt, bsz, seq)
        ssm = _ssd(z, xbc, sm, smt, conv_w[l], conv_b[l], dt_bias[l], a_log[l], d_skip[l], ssm_norm_w[l],
                   bsz, seq)
        x1 = _mix(attn, ssm, gl, x2, w_proj_attn[l].astype(BF16), w_proj_ssm[l].astype(BF16),
                  w_out[l].astype(BF16), b_gates[l], ln1_g[l], ln1_b[l])
        x2 = _ffn(x1, w_ffn_gate[l].astype(BF16), w_ffn_up[l].astype(BF16), w_ffn_down[l].astype(BF16),
                  ln2_g[l], ln2_b[l])
    return x2.reshape(bsz, seq, dm)
```

```python
import functools
import math

import numpy as np
import jax
import jax.numpy as jnp
from jax import lax
from jax.experimental import pallas as pl
from jax.experimental.pallas import tpu as pltpu

F32 = jnp.float32
BF16 = jnp.bfloat16

D_MODEL = 1024
ATT_HEADS = 16
ATT_HEAD_DIM = 64
ATT_WIDTH = ATT_HEADS * ATT_HEAD_DIM
SSM_INNER = 2048
SSM_HEAD_DIM = 64
SSM_HEADS = 32
SSM_GROUPS = 4
SSM_STATE = 128
SSM_CONV = 4
SSM_CHUNK = 128
SSM_CONV_DIM = SSM_INNER + 2 * SSM_GROUPS * SSM_STATE
GROUP_WIDTH = SSM_INNER // SSM_GROUPS
FFN_HIDDEN = 2816
DEPTH = 1
DEEPNORM_ALPHA = (2 * DEPTH) ** 0.25
LN_EPS = 1e-5
RMS_EPS = 1e-5
IN_SIZES = (ATT_WIDTH, ATT_WIDTH, ATT_WIDTH, ATT_HEADS, SSM_INNER, SSM_CONV_DIM, SSM_HEADS, 2 * D_MODEL)
IN_OFFS = tuple(int(v) for v in np.concatenate([[0], np.cumsum(IN_SIZES)]))

LANES = 128
VMEM_LIMIT = 56 * 1024 * 1024

TM_INPROJ = 256
ATT_BLOCK = 512
TM_MIX = 512
TM_FFN = 512
SMALL_W = 128
SMALL_T = 64
DT_OFF = 0
F_OFF = 32
XQ_ROWS = 16
ACC_ROWS = ATT_HEAD_DIM + 16
ATT_ROW_CHUNK = 64
CONV_PAD = 8
LOG2E = math.log2(math.e)
NEG_BIG = -1e30


def _sigmoid(v):
    return 1.0 / (1.0 + jnp.exp(-v))


def _softplus(v):
    return jnp.maximum(v, 0.0) + jnp.log1p(jnp.exp(-jnp.abs(v)))


def _split2(v):
    hi = v.astype(BF16)
    lo = (v - hi.astype(F32)).astype(BF16)
    return hi, lo


def _split3(v):
    hi = v.astype(BF16)
    r = v - hi.astype(F32)
    lo = r.astype(BF16)
    lolo = (r - lo.astype(F32)).astype(BF16)
    return hi, lo, lolo


def _scan_rows(v):
    n = v.shape[0]
    row = lax.broadcasted_iota(jnp.int32, v.shape, 0)
    d = 1
    while d < n:
        v = v + jnp.where(row >= d, pltpu.roll(v, d, 0), 0.0)
        d *= 2
    return v


def _scan_lanes(v):
    n = v.shape[1]
    col = lax.broadcasted_iota(jnp.int32, v.shape, 1)
    d = 1
    while d < n:
        v = v + jnp.where(col >= d, pltpu.roll(v, d, 1), 0.0)
        d *= 2
    return v


def _layer_norm(y, g, b):
    mu = jnp.mean(y, axis=-1, keepdims=True)
    yc = y - mu
    var = jnp.mean(yc * yc, axis=-1, keepdims=True)
    return yc * lax.rsqrt(var + LN_EPS) * g + b


def _const_spec(shape):
    nd = len(shape)
    return pl.BlockSpec(shape, lambda *_: (0,) * nd, pipeline_mode=pl.Buffered(1))


def _params(*sem):
    return pltpu.CompilerParams(dimension_semantics=sem, vmem_limit_bytes=VMEM_LIMIT)


WA_K = (0, 1024)
WA_Z = (1024, 3072)
WA_XBC = (3072, 6144)
WA_G = (6144, 8192)
WA_S = (8192, 8192 + SMALL_W)
WB_Q = (0, 1024)
WB_V = (1024, 2048)
WB_S = (2048, 2048 + SMALL_T)


def _inproj_kernel(x_ref, wa_ref, wbt_ref, k_ref, z_ref, xbc_ref, gl_ref, sm_ref, qt_ref, vt_ref, smt_ref):
    xb = x_ref[...].astype(BF16)

    def seg(r):
        return jnp.dot(xb, wa_ref[:, r[0]:r[1]], preferred_element_type=F32)

    def segt(r):
        return lax.dot_general(wbt_ref[r[0]:r[1], :], xb, (((1,), (1,)), ((), ())),
                               preferred_element_type=F32)

    k_ref[...] = seg(WA_K).astype(BF16)
    z_ref[...] = seg(WA_Z)
    xbc_ref[...] = seg(WA_XBC)
    gl_ref[...] = seg(WA_G)
    sm_ref[...] = seg(WA_S)
    qt_ref[...] = segt(WB_Q).astype(BF16)
    vt_ref[...] = segt(WB_V).astype(BF16)
    smt_ref[...] = segt(WB_S)


def _inproj(x2, wa, wbt):
    t = x2.shape[0]
    tm = TM_INPROJ
    tok = lambda w: pl.BlockSpec((tm, w), lambda i: (i, 0))
    feat = lambda h: pl.BlockSpec((h, tm), lambda i: (0, i))
    return pl.pallas_call(
        _inproj_kernel,
        grid=(t // tm,),
        in_specs=[tok(D_MODEL), _const_spec(wa.shape), _const_spec(wbt.shape)],
        out_specs=[tok(1024), tok(2048), tok(3072), tok(2048), tok(SMALL_W),
                   feat(1024), feat(1024), feat(SMALL_T)],
        out_shape=[
            jax.ShapeDtypeStruct((t, 1024), BF16),
            jax.ShapeDtypeStruct((t, 2048), F32),
            jax.ShapeDtypeStruct((t, 3072), F32),
            jax.ShapeDtypeStruct((t, 2048), F32),
            jax.ShapeDtypeStruct((t, SMALL_W), F32),
            jax.ShapeDtypeStruct((1024, t), BF16),
            jax.ShapeDtypeStruct((1024, t), BF16),
            jax.ShapeDtypeStruct((SMALL_T, t), F32),
        ],
        compiler_params=_params("arbitrary"),
        name="inproj",
    )(x2, wa, wbt)


def _attnprep_kernel(sm_ref, smt_ref, k_ref, bfrow_ref, bfcol_ref, shi_ref, slo_ref, sll_ref, crow_ref,
                     phi_ref, plo_ref, pll_ref, ka_ref, xq_ref, base_ref, carry_ref):
    i = pl.program_id(1)

    @pl.when(i == 0)
    def _():
        carry_ref[...] = jnp.zeros_like(carry_ref)

    g = sm_ref.shape[0]
    v = sm_ref[...] + bfrow_ref[...]
    lf = -_softplus(-v) * LOG2E
    rk = _scan_rows(lf)
    base_ref[0] = carry_ref[...]
    carry_ref[...] = carry_ref[...] + rk[g - 1:g, :]
    hi, lo, ll = _split3(rk)
    extras = (jnp.dot(hi, shi_ref[...], preferred_element_type=F32)
              + jnp.dot(lo, slo_ref[...], preferred_element_type=F32)
              + jnp.dot(ll, sll_ref[...], preferred_element_type=F32)
              + crow_ref[...])
    lane = lax.broadcasted_iota(jnp.int32, (g, LANES), 1)
    low = lane < ATT_HEAD_DIM
    for j in range(ATT_HEADS // 2):
        kk = k_ref[:, LANES * j:LANES * (j + 1)].astype(F32)
        ev = extras[:, 2 * LANES * j:2 * LANES * j + LANES]
        od = extras[:, 2 * LANES * j + LANES:2 * LANES * (j + 1)]
        ka_ref[:, 2 * LANES * j:2 * LANES * j + LANES] = jnp.where(low, kk, ev).astype(BF16)
        ka_ref[:, 2 * LANES * j + LANES:2 * LANES * (j + 1)] = jnp.where(low, od, kk).astype(BF16)

    vt = smt_ref[...] + bfcol_ref[...]
    lft = -_softplus(-vt) * LOG2E
    rq = _scan_lanes(lft)
    hq, lq, llq = _split3(rq)
    nrow = ATT_HEADS * XQ_ROWS
    row = lax.broadcasted_iota(jnp.int32, (nrow, g), 0)
    ones = jnp.where((row & (XQ_ROWS - 1)) < 3, 1.0, 0.0)
    xq = (jnp.dot(phi_ref[...], hq, preferred_element_type=F32)
          + jnp.dot(plo_ref[...], lq, preferred_element_type=F32)
          + jnp.dot(pll_ref[...], llq, preferred_element_type=F32)
          + ones)
    xq_ref[...] = xq.astype(BF16)


def _attnprep_consts():
    s_hi = np.zeros((SMALL_W, ATT_HEADS * LANES), np.float32)
    s_lo = np.zeros_like(s_hi)
    s_ll = np.zeros_like(s_hi)
    crow = np.zeros((1, ATT_HEADS * LANES), np.float32)
    p_hi = np.zeros((ATT_HEADS * XQ_ROWS, SMALL_T), np.float32)
    p_lo = np.zeros_like(p_hi)
    p_ll = np.zeros_like(p_hi)
    for h in range(ATT_HEADS):
        off = LANES * h + (ATT_HEAD_DIM if h % 2 == 0 else 0)
        s_hi[F_OFF + h, off + 0] = -1.0
        s_lo[F_OFF + h, off + 1] = -1.0
        s_ll[F_OFF + h, off + 2] = -1.0
        crow[0, off + 3:off + 6] = 1.0
        p_hi[XQ_ROWS * h + 3, F_OFF + h] = 1.0
        p_lo[XQ_ROWS * h + 4, F_OFF + h] = 1.0
        p_ll[XQ_ROWS * h + 5, F_OFF + h] = 1.0
    b = lambda a: jnp.asarray(a, BF16)
    return b(s_hi), b(s_lo), b(s_ll), jnp.asarray(crow), b(p_hi), b(p_lo), b(p_ll)


def _attnprep(sm, smt, k, b_forget, bsz, seq):
    g = ATT_BLOCK
    nb = seq // g
    s_hi, s_lo, s_ll, crow, p_hi, p_lo, p_ll = _attnprep_consts()
    bfrow = jnp.zeros((1, SMALL_W), F32).at[0, F_OFF:F_OFF + ATT_HEADS].set(b_forget.astype(F32))
    bfcol = jnp.zeros((SMALL_T, 1), F32).at[F_OFF:F_OFF + ATT_HEADS, 0].set(b_forget.astype(F32))
    bfcol = jnp.broadcast_to(bfcol, (SMALL_T, g))
    t = bsz * seq
    ka, xq, base = pl.pallas_call(
        _attnprep_kernel,
        grid=(bsz, nb),
        in_specs=[
            pl.BlockSpec((g, SMALL_W), lambda b, i: (b * nb + i, 0)),
            pl.BlockSpec((SMALL_T, g), lambda b, i: (0, b * nb + i)),
            pl.BlockSpec((g, ATT_WIDTH), lambda b, i: (b * nb + i, 0)),
            _const_spec(bfrow.shape), _const_spec(bfcol.shape),
            _const_spec(s_hi.shape), _const_spec(s_lo.shape), _const_spec(s_ll.shape), _const_spec(crow.shape),
            _const_spec(p_hi.shape), _const_spec(p_lo.shape), _const_spec(p_ll.shape),
        ],
        out_specs=[
            pl.BlockSpec((g, ATT_HEADS * LANES), lambda b, i: (b * nb + i, 0)),
            pl.BlockSpec((ATT_HEADS * XQ_ROWS, g), lambda b, i: (0, b * nb + i)),
            pl.BlockSpec((1, 1, SMALL_W), lambda b, i: (b * nb + i, 0, 0)),
        ],
        out_shape=[
            jax.ShapeDtypeStruct((t, ATT_HEADS * LANES), BF16),
            jax.ShapeDtypeStruct((ATT_HEADS * XQ_ROWS, t), BF16),
            jax.ShapeDtypeStruct((bsz * nb, 1, SMALL_W), F32),
        ],
        scratch_shapes=[pltpu.VMEM((1, SMALL_W), F32)],
        compiler_params=_params("arbitrary", "arbitrary"),
        name="attnprep",
    )(sm, smt, k, bfrow, bfcol, s_hi, s_lo, s_ll, crow, p_hi, p_lo, p_ll)
    base = base.reshape(bsz, nb, SMALL_W)[:, :, F_OFF:F_OFF + ATT_HEADS]
    base = jnp.transpose(base, (0, 2, 1)).reshape(-1)
    return ka, xq, base


def _attn_kernel(base_ref, qt_ref, xq_ref, ka_ref, vt_ref, o_ref, s0_ref, s1_ref, p_ref, acc_ref, *, nb):
    b = pl.program_id(0)
    pr = pl.program_id(1)
    i = pl.program_id(2)
    tq = qt_ref.shape[1]
    tk = tq
    dh = ATT_HEAD_DIM
    zpad = jnp.zeros((LANES - dh - XQ_ROWS, tq), BF16)
    qa = (jnp.concatenate([qt_ref[0:dh, :], xq_ref[0:XQ_ROWS, :], zpad], axis=0),
          jnp.concatenate([xq_ref[XQ_ROWS:2 * XQ_ROWS, :], zpad, qt_ref[dh:2 * dh, :]], axis=0))
    boff = [(b * ATT_HEADS + 2 * pr + e) * nb for e in range(2)]
    ones_rows = jnp.ones((ACC_ROWS - dh, tk), BF16)
    acc_ref[...] = jnp.zeros_like(acc_ref)

    def scores(j, s_ref):
        off = pl.multiple_of(j * tk, tk)
        mts = []
        for e in range(2):
            kj = ka_ref[pl.ds(off, tk), LANES * e:LANES * (e + 1)]
            s = jnp.dot(kj, qa[e], preferred_element_type=F32)
            s_ref[e] = s
            mts.append(jnp.max(s, axis=0, keepdims=True))
        return tuple(mts)

    def accumulate(j, s_ref, mts, ms, masked):
        off = pl.multiple_of(j * tk, tk)
        new_ms = []
        for e in range(2):
            c = base_ref[boff[e] + i] - base_ref[boff[e] + j]
            if masked:
                mt = None
                for r in range(tk // ATT_ROW_CHUNK):
                    rows = slice(r * ATT_ROW_CHUNK, (r + 1) * ATT_ROW_CHUNK)
                    krow = lax.broadcasted_iota(jnp.int32, (ATT_ROW_CHUNK, tq), 0) + r * ATT_ROW_CHUNK
                    qcol = lax.broadcasted_iota(jnp.int32, (ATT_ROW_CHUNK, tq), 1)
                    sc = jnp.where(krow <= qcol, s_ref[e, rows, :], NEG_BIG)
                    s_ref[e, rows, :] = sc
                    cm = jnp.max(sc, axis=0, keepdims=True)
                    mt = cm if mt is None else jnp.maximum(mt, cm)
            else:
                mt = mts[e]
            m_new = jnp.maximum(ms[e], mt + c)
            alpha = jnp.exp2(ms[e] - m_new)
            sh = m_new - c
            for r in range(tk // ATT_ROW_CHUNK):
                rows = slice(r * ATT_ROW_CHUNK, (r + 1) * ATT_ROW_CHUNK)
                p_ref[e, rows, :] = jnp.exp2(s_ref[e, rows, :] - sh).astype(BF16)
            vj = vt_ref[dh * e:dh * (e + 1), pl.ds(off, tk)]
            vaug = jnp.concatenate([vj, ones_rows], axis=0)
            acc_ref[e] = alpha * acc_ref[e] + jnp.dot(vaug, p_ref[e], preferred_element_type=F32)
            new_ms.append(m_new)
        return tuple(new_ms)

    def finish():
        ot = jnp.concatenate([acc_ref[e, 0:dh, :] / acc_ref[e, dh:dh + 1, :] for e in range(2)], axis=0)
        o_ref[...] = ot.T.astype(o_ref.dtype)

    def pair(t, carry):
        mts, ms = carry
        j = 2 * t
        mts1 = scores(j + 1, s1_ref)
        ms = accumulate(j, s0_ref, mts, ms, False)
        mts0 = scores(j + 2, s0_ref)
        ms = accumulate(j + 1, s1_ref, mts1, ms, False)
        return mts0, ms

    ms0 = tuple(jnp.full((1, tq), NEG_BIG, F32) for _ in range(2))
    mts, ms = lax.fori_loop(0, i // 2, pair, (scores(0, s0_ref), ms0))

    @pl.when(i % 2 == 0)
    def _():
        accumulate(i, s0_ref, None, ms, True)
        finish()

    @pl.when(i % 2 == 1)
    def _():
        scores(i, s1_ref)
        ms_ = accumulate(i - 1, s0_ref, mts, ms, False)
        accumulate(i, s1_ref, None, ms_, True)
        finish()


def _attention(base, qt, xq, ka, vt, bsz, seq):
    tq = ATT_BLOCK
    nb = seq // tq
    npair = ATT_HEADS // 2
    t = bsz * seq
    return pl.pallas_call(
        functools.partial(_attn_kernel, nb=nb),
        grid=(bsz, npair, nb),
        in_specs=[
            pl.BlockSpec(memory_space=pltpu.SMEM),
            pl.BlockSpec((2 * ATT_HEAD_DIM, tq), lambda b, p, i: (p, b * nb + i)),
            pl.BlockSpec((2 * XQ_ROWS, tq), lambda b, p, i: (p, b * nb + i)),
            pl.BlockSpec((seq, 2 * LANES), lambda b, p, i: (b, p)),
            pl.BlockSpec((2 * ATT_HEAD_DIM, seq), lambda b, p, i: (p, b)),
        ],
        out_specs=pl.BlockSpec((tq, 2 * ATT_HEAD_DIM), lambda b, p, i: (b * nb + i, p)),
        out_shape=jax.ShapeDtypeStruct((t, ATT_WIDTH), BF16),
        scratch_shapes=[pltpu.VMEM((2, tq, tq), F32), pltpu.VMEM((2, tq, tq), F32),
                        pltpu.VMEM((2, tq, tq), BF16), pltpu.VMEM((2, ACC_ROWS, tq), F32)],
        compiler_params=_params("arbitrary", "arbitrary", "arbitrary"),
        name="attn",
    )(base, qt, xq, ka, vt)


def _ssd_kernel(z_ref, xbc_ref, sm_ref, smt_ref, cw_ref, cb_ref, dtbrow_ref, dtbcol_ref, arow_ref, acol_ref,
                dexp_ref, nw_ref, e2_ref, o_ref, prev_ref, state_ref):
    c = pl.program_id(1)
    L = SSM_CHUNK
    N = SSM_STATE
    GW = GROUP_WIDTH

    @pl.when(c == 0)
    def _():
        prev_ref[0:CONV_PAD, :] = jnp.zeros((CONV_PAD, SSM_CONV_DIM), F32)
        state_ref[...] = jnp.zeros_like(state_ref)

    u = xbc_ref[...]
    prev_ref[CONV_PAD:CONV_PAD + L, :] = u
    acc = u * cw_ref[SSM_CONV - 1:SSM_CONV, :] + cb_ref[...]
    for s in range(1, SSM_CONV):
        acc = acc + prev_ref[CONV_PAD - s:CONV_PAD - s + L, :] * cw_ref[SSM_CONV - 1 - s:SSM_CONV - s, :]
    prev_ref[0:CONV_PAD, :] = u[L - CONV_PAD:L, :]
    xc = acc * _sigmoid(acc)
    xs = xc[:, :SSM_INNER]
    bm = xc[:, SSM_INNER:SSM_INNER + SSM_GROUPS * N].astype(BF16)
    cm = xc[:, SSM_INNER + SSM_GROUPS * N:].astype(BF16)

    a_row = -jnp.exp(arow_ref[...])
    dt = _softplus(sm_ref[...] + dtbrow_ref[...])
    acum = _scan_rows(dt * a_row)
    a_col = -jnp.exp(acol_ref[...])
    dtt = _softplus(smt_ref[...] + dtbcol_ref[...])
    acumt = _scan_lanes(dtt * a_col)
    alast = acum[L - 1:L, :]
    ea = jnp.exp(acum)
    w2 = dt * jnp.exp(alast - acum)

    def expand(v):
        hi, lo = _split2(v)
        return jnp.dot(jnp.concatenate([hi, lo], axis=1), e2_ref[...], preferred_element_type=F32)

    dt_e = expand(dt)
    ea_e = expand(ea)
    w2_e = expand(w2)
    cd_e = ea_e[L - 1:L, :]

    xdt = (xs * dt_e).astype(BF16)
    xw2 = (xs * w2_e).astype(BF16)

    r2 = lax.broadcasted_iota(jnp.int32, (L, L), 0)
    c2 = lax.broadcasted_iota(jnp.int32, (L, L), 1)
    tri = r2 >= c2
    lane = lax.broadcasted_iota(jnp.int32, (L, LANES), 1)
    low = lane < SSM_HEAD_DIM

    ys = []
    for g in range(SSM_GROUPS):
        bg = bm[:, N * g:N * (g + 1)]
        cg = cm[:, N * g:N * (g + 1)]
        cb = lax.dot_general(cg, bg, (((1,), (1,)), ((), ())), preferred_element_type=F32)
        st_prev = state_ref[g]
        y_off = jnp.dot(cg, st_prev.astype(BF16), preferred_element_type=F32) * ea_e[:, GW * g:GW * (g + 1)]
        st_c = lax.dot_general(bg, xw2[:, GW * g:GW * (g + 1)], (((0,), (0,)), ((), ())),
                               preferred_element_type=F32)
        state_ref[g] = st_prev * cd_e[:, GW * g:GW * (g + 1)] + st_c
        yd = []
        for pj in range(GW // LANES):
            h0 = (GW // SSM_HEAD_DIM) * g + 2 * pj
            ms = []
            for h in (h0, h0 + 1):
                diff = acum[:, h:h + 1] - acumt[h:h + 1, :]
                ms.append((cb * jnp.exp(jnp.where(tri, diff, NEG_BIG))).astype(BF16))
            xp = xdt[:, LANES * (h0 // 2):LANES * (h0 // 2 + 1)]
            zero = jnp.zeros_like(xp)
            rhs = jnp.concatenate([jnp.where(low, xp, zero), jnp.where(low, zero, xp)], axis=0)
            yd.append(jnp.dot(jnp.concatenate(ms, axis=1), rhs, preferred_element_type=F32))
        ys.append(jnp.concatenate(yd, axis=1) + y_off)
    y = jnp.concatenate(ys, axis=1) + dexp_ref[...] * xs

    zz = z_ref[...]
    uu = y * (zz * _sigmoid(zz))
    outs = []
    for g in range(SSM_GROUPS):
        ug = uu[:, GW * g:GW * (g + 1)]
        ms_ = jnp.mean(ug * ug, axis=-1, keepdims=True)
        outs.append(ug * lax.rsqrt(ms_ + RMS_EPS))
    o_ref[...] = (jnp.concatenate(outs, axis=1) * nw_ref[...]).astype(o_ref.dtype)


def _ssd(z, xbc, sm, smt, conv_w, conv_b, dt_bias, a_log, d_skip, norm_w, bsz, seq):
    L = SSM_CHUNK
    nc = seq // L
    t = bsz * seq
    dtbrow = jnp.zeros((1, SMALL_W), F32).at[0, DT_OFF:DT_OFF + SSM_HEADS].set(dt_bias.astype(F32))
    arow = jnp.zeros((1, SMALL_W), F32).at[0, DT_OFF:DT_OFF + SSM_HEADS].set(a_log.astype(F32))
    dtbcol = jnp.zeros((SMALL_T, 1), F32).at[DT_OFF:DT_OFF + SSM_HEADS, 0].set(dt_bias.astype(F32))
    acol = jnp.zeros((SMALL_T, 1), F32).at[DT_OFF:DT_OFF + SSM_HEADS, 0].set(a_log.astype(F32))
    dtbcol = jnp.broadcast_to(dtbcol, (SMALL_T, L))
    acol = jnp.broadcast_to(acol, (SMALL_T, L))
    dexp = jnp.repeat(d_skip.astype(F32), SSM_HEAD_DIM)[None, :]
    e1 = np.zeros((SMALL_W, SSM_INNER), np.float32)
    for h in range(SSM_HEADS):
        e1[DT_OFF + h, SSM_HEAD_DIM * h:SSM_HEAD_DIM * (h + 1)] = 1.0
    e2 = jnp.asarray(np.concatenate([e1, e1], axis=0), BF16)
    tok = lambda w: pl.BlockSpec((L, w), lambda b, c: (b * nc + c, 0))
    return pl.pallas_call(
        _ssd_kernel,
        grid=(bsz, nc),
        in_specs=[
            tok(SSM_INNER), tok(SSM_CONV_DIM), tok(SMALL_W),
            pl.BlockSpec((SMALL_T, L), lambda b, c: (0, b * nc + c)),
            _const_spec((SSM_CONV, SSM_CONV_DIM)), _const_spec((1, SSM_CONV_DIM)),
            _const_spec(dtbrow.shape), _const_spec(dtbcol.shape), _const_spec(arow.shape), _const_spec(acol.shape),
            _const_spec(dexp.shape), _const_spec((1, SSM_INNER)), _const_spec(e2.shape),
        ],
        out_specs=tok(SSM_INNER),
        out_shape=jax.ShapeDtypeStruct((t, SSM_INNER), BF16),
        scratch_shapes=[pltpu.VMEM((CONV_PAD + L, SSM_CONV_DIM), F32),
                        pltpu.VMEM((SSM_GROUPS, SSM_STATE, GROUP_WIDTH), F32)],
        compiler_params=_params("arbitrary", "arbitrary"),
        name="ssd",
    )(z, xbc, sm, smt, conv_w.astype(F32), conv_b.astype(F32)[None, :], dtbrow, dtbcol, arow, acol,
      dexp, norm_w.astype(F32)[None, :], e2)


def _mix_kernel(attn_ref, ssm_ref, gl_ref, x_ref, wpa_ref, wps_ref, wout_ref, bg_ref, g_ref, b_ref, o_ref):
    ad = jnp.dot(attn_ref[...], wpa_ref[...], preferred_element_type=F32)
    sd = jnp.dot(ssm_ref[...], wps_ref[...], preferred_element_type=F32)
    gates = _sigmoid(gl_ref[...] + bg_ref[...])
    mix = gates[:, :D_MODEL] * ad + gates[:, D_MODEL:] * sd
    mixed = jnp.dot(mix.astype(BF16), wout_ref[...], preferred_element_type=F32)
    o_ref[...] = _layer_norm(DEEPNORM_ALPHA * x_ref[...] + mixed, g_ref[...], b_ref[...])


def _mix(attn, ssm, gl, x2, wpa, wps, wout, b_gates, ln_g, ln_b):
    t = x2.shape[0]
    tm = TM_MIX
    tok = lambda w: pl.BlockSpec((tm, w), lambda i: (i, 0))
    return pl.pallas_call(
        _mix_kernel,
        grid=(t // tm,),
        in_specs=[tok(ATT_WIDTH), tok(SSM_INNER), tok(2 * D_MODEL), tok(D_MODEL),
                  _const_spec(wpa.shape), _const_spec(wps.shape), _const_spec(wout.shape),
                  _const_spec((1, 2 * D_MODEL)), _const_spec((1, D_MODEL)), _const_spec((1, D_MODEL))],
        out_specs=tok(D_MODEL),
        out_shape=jax.ShapeDtypeStruct((t, D_MODEL), F32),
        compiler_params=_params("arbitrary"),
        name="mix",
    )(attn, ssm, gl, x2, wpa, wps, wout, b_gates.astype(F32)[None, :], ln_g.astype(F32)[None, :],
      ln_b.astype(F32)[None, :])


def _ffn_kernel(x_ref, wg_ref, wu_ref, wd_ref, g_ref, b_ref, o_ref):
    x = x_ref[...]
    xb = x.astype(BF16)
    hg = jnp.dot(xb, wg_ref[...], preferred_element_type=F32)
    hu = jnp.dot(xb, wu_ref[...], preferred_element_type=F32)
    h = (hg * _sigmoid(hg) * hu).astype(BF16)
    d = jnp.dot(h, wd_ref[...], preferred_element_type=F32)
    o_ref[...] = _layer_norm(DEEPNORM_ALPHA * x + d, g_ref[...], b_ref[...])


def _ffn(x1, wg, wu, wd, ln_g, ln_b):
    t = x1.shape[0]
    tm = TM_FFN
    tok = pl.BlockSpec((tm, D_MODEL), lambda i: (i, 0))
    return pl.pallas_call(
        _ffn_kernel,
        grid=(t // tm,),
        in_specs=[tok, _const_spec(wg.shape), _const_spec(wu.shape), _const_spec(wd.shape),
                  _const_spec((1, D_MODEL)), _const_spec((1, D_MODEL))],
        out_specs=tok,
        out_shape=jax.ShapeDtypeStruct((t, D_MODEL), F32),
        compiler_params=_params("arbitrary"),
        name="ffn",
    )(x1, wg, wu, wd, ln_g.astype(F32)[None, :], ln_b.astype(F32)[None, :])


def _pack_in_weights(w):
    o = IN_OFFS
    q, k, v, f = w[:, o[0]:o[1]], w[:, o[1]:o[2]], w[:, o[2]:o[3]], w[:, o[3]:o[4]]
    z, xbc, dt, gate = w[:, o[4]:o[5]], w[:, o[5]:o[6]], w[:, o[6]:o[7]], w[:, o[7]:o[8]]
    small = jnp.zeros((D_MODEL, SMALL_W), w.dtype)
    small = small.at[:, DT_OFF:DT_OFF + SSM_HEADS].set(dt).at[:, F_OFF:F_OFF + ATT_HEADS].set(f)
    wa = jnp.concatenate([k, z, xbc, gate, small], axis=1).astype(BF16)
    scale = LOG2E / math.sqrt(ATT_HEAD_DIM)
    wbt = jnp.concatenate([(q * scale).T, v.T, small[:, :SMALL_T].T], axis=0).astype(BF16)
    return wa, wbt


def kernel(x, w_in, b_forget, conv_w, conv_b, dt_bias, a_log, d_skip, ssm_norm_w, w_proj_attn,
           w_proj_ssm, b_gates, w_out, ln1_g, ln1_b, w_ffn_gate, w_ffn_up, w_ffn_down, ln2_g, ln2_b):
    bsz, seq, dm = x.shape
    assert dm == D_MODEL and seq % ATT_BLOCK == 0 and seq % SSM_CHUNK == 0
    assert w_in.shape[0] == DEPTH
    x2 = x.reshape(bsz * seq, dm)
    for l in range(DEPTH):
        wa, wbt = _pack_in_weights(w_in[l])
        k, z, xbc, gl, sm, qt, vt, smt = _inproj(x2, wa, wbt)
        ka, xq, base = _attnprep(sm, smt, k, b_forget[l], bsz, seq)
        attn = _attention(base, qt, xq, ka, vt, bsz, seq)
        ssm = _ssd(z, xbc, sm, smt, conv_w[l], conv_b[l], dt_bias[l], a_log[l], d_skip[l], ssm_norm_w[l],
                   bsz, seq)
        x1 = _mix(attn, ssm, gl, x2, w_proj_attn[l].astype(BF16), w_proj_ssm[l].astype(BF16),
                  w_out[l].astype(BF16), b_gates[l], ln1_g[l], ln1_b[l])
        x2 = _ffn(x1, w_ffn_gate[l].astype(BF16), w_ffn_up[l].astype(BF16), w_ffn_down[l].astype(BF16),
                  ln2_g[l], ln2_b[l])
    return x2.reshape(bsz, seq, dm)
```

```python
import functools
import math

import numpy as np
import jax
import jax.numpy as jnp
from jax import lax
from jax.experimental import pallas as pl
from jax.experimental.pallas import tpu as pltpu

F32 = jnp.float32
BF16 = jnp.bfloat16

D_MODEL = 1024
ATT_HEADS = 16
ATT_HEAD_DIM = 64
ATT_WIDTH = ATT_HEADS * ATT_HEAD_DIM
SSM_INNER = 2048
SSM_HEAD_DIM = 64
SSM_HEADS = 32
SSM_GROUPS = 4
SSM_STATE = 128
SSM_CONV = 4
SSM_CHUNK = 128
SSM_CONV_DIM = SSM_INNER + 2 * SSM_GROUPS * SSM_STATE
GROUP_WIDTH = SSM_INNER // SSM_GROUPS
FFN_HIDDEN = 2816
DEPTH = 1
DEEPNORM_ALPHA = (2 * DEPTH) ** 0.25
LN_EPS = 1e-5
RMS_EPS = 1e-5
IN_SIZES = (ATT_WIDTH, ATT_WIDTH, ATT_WIDTH, ATT_HEADS, SSM_INNER, SSM_CONV_DIM, SSM_HEADS, 2 * D_MODEL)
IN_OFFS = tuple(int(v) for v in np.concatenate([[0], np.cumsum(IN_SIZES)]))

LANES = 128
VMEM_LIMIT = 56 * 1024 * 1024

TM_INPROJ = 256
ATT_BLOCK = 512
ATT_QBLOCK = 1024
TM_MIX = 512
TM_FFN = 512
SMALL_W = 128
SMALL_T = 64
DT_OFF = 0
F_OFF = 32
XQ_ROWS = 16
ACC_ROWS = ATT_HEAD_DIM + 16
ATT_ROW_CHUNK = 64
CONV_PAD = 16
LOG2E = math.log2(math.e)
NEG_BIG = -1e30


def _sigmoid(v):
    return 1.0 / (1.0 + jnp.exp(-v))


def _softplus(v):
    return jnp.maximum(v, 0.0) + jnp.log1p(jnp.exp(-jnp.abs(v)))


def _split2(v):
    hi = v.astype(BF16)
    lo = (v - hi.astype(F32)).astype(BF16)
    return hi, lo


def _split3(v):
    hi = v.astype(BF16)
    r = v - hi.astype(F32)
    lo = r.astype(BF16)
    lolo = (r - lo.astype(F32)).astype(BF16)
    return hi, lo, lolo


def _scan_rows(v):
    n = v.shape[0]
    row = lax.broadcasted_iota(jnp.int32, v.shape, 0)
    d = 1
    while d < n:
        v = v + jnp.where(row >= d, pltpu.roll(v, d, 0), 0.0)
        d *= 2
    return v


def _scan_lanes(v):
    n = v.shape[1]
    col = lax.broadcasted_iota(jnp.int32, v.shape, 1)
    d = 1
    while d < n:
        v = v + jnp.where(col >= d, pltpu.roll(v, d, 1), 0.0)
        d *= 2
    return v


def _layer_norm(y, g, b):
    mu = jnp.mean(y, axis=-1, keepdims=True)
    yc = y - mu
    var = jnp.mean(yc * yc, axis=-1, keepdims=True)
    return yc * lax.rsqrt(var + LN_EPS) * g + b


def _const_spec(shape):
    nd = len(shape)
    return pl.BlockSpec(shape, lambda *_: (0,) * nd, pipeline_mode=pl.Buffered(1))


def _params(*sem):
    return pltpu.CompilerParams(dimension_semantics=sem, vmem_limit_bytes=VMEM_LIMIT)


WA_K = (0, 1024)
WA_Z = (1024, 3072)
WA_XBC = (3072, 6144)
WA_G = (6144, 8192)
WA_S = (8192, 8192 + SMALL_W)
WB_Q = (0, 1024)
WB_V = (1024, 2048)
WB_S = (2048, 2048 + SMALL_T)


def _inproj_kernel(x_ref, wa_ref, wbt_ref, k_ref, z_ref, xbc_ref, gl_ref, sm_ref, qt_ref, vt_ref, smt_ref):
    xb = x_ref[...].astype(BF16)

    def seg(r):
        return jnp.dot(xb, wa_ref[:, r[0]:r[1]], preferred_element_type=F32)

    def segt(r):
        return lax.dot_general(wbt_ref[r[0]:r[1], :], xb, (((1,), (1,)), ((), ())),
                               preferred_element_type=F32)

    k_ref[...] = seg(WA_K).astype(BF16)
    z_ref[...] = seg(WA_Z)
    xbc_ref[...] = seg(WA_XBC).astype(BF16)
    gl_ref[...] = seg(WA_G)
    sm_ref[...] = seg(WA_S)
    qt_ref[...] = segt(WB_Q).astype(BF16)
    vt_ref[...] = segt(WB_V).astype(BF16)
    smt_ref[...] = segt(WB_S)


def _inproj(x2, wa, wbt):
    t = x2.shape[0]
    tm = TM_INPROJ
    tok = lambda w: pl.BlockSpec((tm, w), lambda i: (i, 0))
    feat = lambda h: pl.BlockSpec((h, tm), lambda i: (0, i))
    return pl.pallas_call(
        _inproj_kernel,
        grid=(t // tm,),
        in_specs=[tok(D_MODEL), _const_spec(wa.shape), _const_spec(wbt.shape)],
        out_specs=[tok(1024), tok(2048), tok(3072), tok(2048), tok(SMALL_W),
                   feat(1024), feat(1024), feat(SMALL_T)],
        out_shape=[
            jax.ShapeDtypeStruct((t, 1024), BF16),
            jax.ShapeDtypeStruct((t, 2048), F32),
            jax.ShapeDtypeStruct((t, 3072), BF16),
            jax.ShapeDtypeStruct((t, 2048), F32),
            jax.ShapeDtypeStruct((t, SMALL_W), F32),
            jax.ShapeDtypeStruct((1024, t), BF16),
            jax.ShapeDtypeStruct((1024, t), BF16),
            jax.ShapeDtypeStruct((SMALL_T, t), F32),
        ],
        compiler_params=_params("arbitrary"),
        name="inproj",
    )(x2, wa, wbt)


def _attnprep_kernel(sm_ref, smt_ref, k_ref, bfrow_ref, bfcol_ref, shi_ref, slo_ref, sll_ref, crow_ref,
                     phi_ref, plo_ref, pll_ref, ka_ref, xq_ref, base_ref, carry_ref):
    i = pl.program_id(1)

    @pl.when(i == 0)
    def _():
        carry_ref[...] = jnp.zeros_like(carry_ref)

    g = sm_ref.shape[0]
    v = sm_ref[...] + bfrow_ref[...]
    lf = -_softplus(-v) * LOG2E
    rk = _scan_rows(lf)
    base_ref[0] = carry_ref[...]
    carry_ref[...] = carry_ref[...] + rk[g - 1:g, :]
    hi, lo, ll = _split3(rk)
    extras = (jnp.dot(hi, shi_ref[...], preferred_element_type=F32)
              + jnp.dot(lo, slo_ref[...], preferred_element_type=F32)
              + jnp.dot(ll, sll_ref[...], preferred_element_type=F32)
              + crow_ref[...])
    lane = lax.broadcasted_iota(jnp.int32, (g, LANES), 1)
    low = lane < ATT_HEAD_DIM
    for j in range(ATT_HEADS // 2):
        kk = k_ref[:, LANES * j:LANES * (j + 1)].astype(F32)
        ev = extras[:, 2 * LANES * j:2 * LANES * j + LANES]
        od = extras[:, 2 * LANES * j + LANES:2 * LANES * (j + 1)]
        ka_ref[:, 2 * LANES * j:2 * LANES * j + LANES] = jnp.where(low, kk, ev).astype(BF16)
        ka_ref[:, 2 * LANES * j + LANES:2 * LANES * (j + 1)] = jnp.where(low, od, kk).astype(BF16)

    vt = smt_ref[...] + bfcol_ref[...]
    lft = -_softplus(-vt) * LOG2E
    rq = _scan_lanes(lft)
    hq, lq, llq = _split3(rq)
    nrow = ATT_HEADS * XQ_ROWS
    row = lax.broadcasted_iota(jnp.int32, (nrow, g), 0)
    ones = jnp.where((row & (XQ_ROWS - 1)) < 3, 1.0, 0.0)
    xq = (jnp.dot(phi_ref[...], hq, preferred_element_type=F32)
          + jnp.dot(plo_ref[...], lq, preferred_element_type=F32)
          + jnp.dot(pll_ref[...], llq, preferred_element_type=F32)
          + ones)
    xq_ref[...] = xq.astype(BF16)


def _attnprep_consts():
    s_hi = np.zeros((SMALL_W, ATT_HEADS * LANES), np.float32)
    s_lo = np.zeros_like(s_hi)
    s_ll = np.zeros_like(s_hi)
    crow = np.zeros((1, ATT_HEADS * LANES), np.float32)
    p_hi = np.zeros((ATT_HEADS * XQ_ROWS, SMALL_T), np.float32)
    p_lo = np.zeros_like(p_hi)
    p_ll = np.zeros_like(p_hi)
    for h in range(ATT_HEADS):
        off = LANES * h + (ATT_HEAD_DIM if h % 2 == 0 else 0)
        s_hi[F_OFF + h, off + 0] = -1.0
        s_lo[F_OFF + h, off + 1] = -1.0
        s_ll[F_OFF + h, off + 2] = -1.0
        crow[0, off + 3:off + 6] = 1.0
        p_hi[XQ_ROWS * h + 3, F_OFF + h] = 1.0
        p_lo[XQ_ROWS * h + 4, F_OFF + h] = 1.0
        p_ll[XQ_ROWS * h + 5, F_OFF + h] = 1.0
    b = lambda a: jnp.asarray(a, BF16)
    return b(s_hi), b(s_lo), b(s_ll), jnp.asarray(crow), b(p_hi), b(p_lo), b(p_ll)


def _attnprep(sm, smt, k, b_forget, bsz, seq):
    g = ATT_BLOCK
    nb = seq // g
    s_hi, s_lo, s_ll, crow, p_hi, p_lo, p_ll = _attnprep_consts()
    bfrow = jnp.zeros((1, SMALL_W), F32).at[0, F_OFF:F_OFF + ATT_HEADS].set(b_forget.astype(F32))
    bfcol = jnp.zeros((SMALL_T, 1), F32).at[F_OFF:F_OFF + ATT_HEADS, 0].set(b_forget.astype(F32))
    bfcol = jnp.broadcast_to(bfcol, (SMALL_T, g))
    t = bsz * seq
    ka, xq, base = pl.pallas_call(
        _attnprep_kernel,
        grid=(bsz, nb),
        in_specs=[
            pl.BlockSpec((g, SMALL_W), lambda b, i: (b * nb + i, 0)),
            pl.BlockSpec((SMALL_T, g), lambda b, i: (0, b * nb + i)),
            pl.BlockSpec((g, ATT_WIDTH), lambda b, i: (b * nb + i, 0)),
            _const_spec(bfrow.shape), _const_spec(bfcol.shape),
            _const_spec(s_hi.shape), _const_spec(s_lo.shape), _const_spec(s_ll.shape), _const_spec(crow.shape),
            _const_spec(p_hi.shape), _const_spec(p_lo.shape), _const_spec(p_ll.shape),
        ],
        out_specs=[
            pl.BlockSpec((g, ATT_HEADS * LANES), lambda b, i: (b * nb + i, 0)),
            pl.BlockSpec((ATT_HEADS * XQ_ROWS, g), lambda b, i: (0, b * nb + i)),
            pl.BlockSpec((1, 1, SMALL_W), lambda b, i: (b * nb + i, 0, 0)),
        ],
        out_shape=[
            jax.ShapeDtypeStruct((t, ATT_HEADS * LANES), BF16),
            jax.ShapeDtypeStruct((ATT_HEADS * XQ_ROWS, t), BF16),
            jax.ShapeDtypeStruct((bsz * nb, 1, SMALL_W), F32),
        ],
        scratch_shapes=[pltpu.VMEM((1, SMALL_W), F32)],
        compiler_params=_params("arbitrary", "arbitrary"),
        name="attnprep",
    )(sm, smt, k, bfrow, bfcol, s_hi, s_lo, s_ll, crow, p_hi, p_lo, p_ll)
    base = base.reshape(bsz, nb, SMALL_W)[:, :, F_OFF:F_OFF + ATT_HEADS]
    base = jnp.transpose(base, (0, 2, 1)).reshape(-1)
    return ka, xq, base


def _attn_kernel(base_ref, qt_ref, xq_ref, ka_ref, vt_ref, o_ref,
                 s00_ref, s01_ref, s10_ref, s11_ref, p0_ref, p1_ref, a0_ref, a1_ref, *, nb):
    b = pl.program_id(0)
    pr = pl.program_id(1)
    i = pl.program_id(2)
    tq = qt_ref.shape[1]
    tk = ATT_BLOCK
    dh = ATT_HEAD_DIM
    s_slots = ((s00_ref, s01_ref), (s10_ref, s11_ref))
    p_refs = (p0_ref, p1_ref)
    acc_refs = (a0_ref, a1_ref)
    zpad = jnp.zeros((LANES - dh - XQ_ROWS, tq), BF16)
    qa = (jnp.concatenate([qt_ref[0:dh, :], xq_ref[0:XQ_ROWS, :], zpad], axis=0),
          jnp.concatenate([xq_ref[XQ_ROWS:2 * XQ_ROWS, :], zpad, qt_ref[dh:2 * dh, :]], axis=0))
    boff = [(b * ATT_HEADS + 2 * pr + e) * nb for e in range(2)]
    ones_rows = jnp.ones((ACC_ROWS - dh, tk), BF16)
    for a_ref in acc_refs:
        a_ref[...] = jnp.zeros_like(a_ref)
    qlane = lax.broadcasted_iota(jnp.int32, (1, tq), 1)
    qbase = [jnp.where(qlane >= tk, base_ref[boff[e] + 2 * i + 1], base_ref[boff[e] + 2 * i]) for e in range(2)]

    def scores(j, s_ref, e, lo):
        off = pl.multiple_of(j * tk, tk)
        kj = ka_ref[pl.ds(off, tk), LANES * e:LANES * (e + 1)]
        s = jnp.dot(kj, qa[e][:, lo:], preferred_element_type=F32)
        s_ref[:, lo:] = s
        return jnp.max(s, axis=0, keepdims=True)

    def accumulate(j, s_ref, mt, m, e, lo, masked):
        off = pl.multiple_of(j * tk, tk)
        w = tq - lo
        c = qbase[e][:, lo:] - base_ref[boff[e] + j]
        p_ref, acc_ref = p_refs[e], acc_refs[e]
        if masked:
            mt = None
            for r in range(tk // ATT_ROW_CHUNK):
                rows = slice(r * ATT_ROW_CHUNK, (r + 1) * ATT_ROW_CHUNK)
                krow = lax.broadcasted_iota(jnp.int32, (ATT_ROW_CHUNK, w), 0) + r * ATT_ROW_CHUNK
                qcol = lax.broadcasted_iota(jnp.int32, (ATT_ROW_CHUNK, w), 1)
                sc = jnp.where(krow <= qcol, s_ref[rows, lo:], NEG_BIG)
                s_ref[rows, lo:] = sc
                cm = jnp.max(sc, axis=0, keepdims=True)
                mt = cm if mt is None else jnp.maximum(mt, cm)
        m_old = m[:, lo:]
        m_new = jnp.maximum(m_old, mt + c)
        alpha = jnp.exp2(m_old - m_new)
        sh = m_new - c
        for r in range(tk // ATT_ROW_CHUNK):
            rows = slice(r * ATT_ROW_CHUNK, (r + 1) * ATT_ROW_CHUNK)
            p_ref[rows, lo:] = jnp.exp2(s_ref[rows, lo:] - sh).astype(BF16)
        vj = vt_ref[dh * e:dh * (e + 1), pl.ds(off, tk)]
        vaug = jnp.concatenate([vj, ones_rows], axis=0)
        acc_ref[:, lo:] = alpha * acc_ref[:, lo:] + jnp.dot(vaug, p_ref[:, lo:], preferred_element_type=F32)
        return m_new if lo == 0 else jnp.concatenate([m[:, :lo], m_new], axis=1)

    def pair(t, ms, diag=False):
        los = [d * tk if diag else 0 for d in range(2)]
        mts = [[scores(2 * t + d, s_slots[d][e], e, los[d]) for e in range(2)] for d in range(2)]
        for d in range(2):
            ms = tuple(accumulate(2 * t + d, s_slots[d][e], mts[d][e], ms[e], e, los[d], diag) for e in range(2))
        return ms

    ms0 = tuple(jnp.full((1, tq), NEG_BIG, F32) for _ in range(2))
    ms = lax.fori_loop(0, i, pair, ms0)
    pair(i, ms, diag=True)
    ot = jnp.concatenate([a_ref[0:dh, :] / a_ref[dh:dh + 1, :] for a_ref in acc_refs], axis=0)
    o_ref[...] = ot.T.astype(o_ref.dtype)


def _attention(base, qt, xq, ka, vt, bsz, seq):
    tq = ATT_QBLOCK
    tk = ATT_BLOCK
    assert tq == 2 * tk and seq % tq == 0
    nq = seq // tq
    nb = seq // tk
    npair = ATT_HEADS // 2
    t = bsz * seq
    return pl.pallas_call(
        functools.partial(_attn_kernel, nb=nb),
        grid=(bsz, npair, nq),
        in_specs=[
            pl.BlockSpec(memory_space=pltpu.SMEM),
            pl.BlockSpec((2 * ATT_HEAD_DIM, tq), lambda b, p, i: (p, b * nq + i)),
            pl.BlockSpec((2 * XQ_ROWS, tq), lambda b, p, i: (p, b * nq + i)),
            pl.BlockSpec((seq, 2 * LANES), lambda b, p, i: (b, p)),
            pl.BlockSpec((2 * ATT_HEAD_DIM, seq), lambda b, p, i: (p, b)),
        ],
        out_specs=pl.BlockSpec((tq, 2 * ATT_HEAD_DIM), lambda b, p, i: (b * nq + i, p)),
        out_shape=jax.ShapeDtypeStruct((t, ATT_WIDTH), BF16),
        scratch_shapes=([pltpu.VMEM((tk, tq), F32)] * 4 + [pltpu.VMEM((tk, tq), BF16)] * 2
                        + [pltpu.VMEM((ACC_ROWS, tq), F32)] * 2),
        compiler_params=_params("arbitrary", "arbitrary", "arbitrary"),
        name="attn",
    )(base, qt, xq, ka, vt)


def _ssd_kernel(z_ref, xbc_ref, sm_ref, smt_ref, cw_ref, cwb_ref, cb_ref, shift_ref, dtbrow_ref, dtbcol_ref,
                arow_ref, acol_ref, dexp_ref, nw_ref, e2_ref, o_ref, prev_ref, state_ref):
    c = pl.program_id(1)
    L = SSM_CHUNK
    N = SSM_STATE
    GW = GROUP_WIDTH

    @pl.when(c == 0)
    def _():
        prev_ref[0:CONV_PAD, :] = jnp.zeros((CONV_PAD, SSM_CONV_DIM), BF16)
        state_ref[...] = jnp.zeros_like(state_ref)

    ub = xbc_ref[...]
    prev_ref[CONV_PAD:CONV_PAD + L, :] = ub
    win = prev_ref[...]
    taps = jnp.concatenate([win * cwb_ref[SSM_CONV - 1 - s:SSM_CONV - s, :] for s in range(1, SSM_CONV)],
                           axis=0)
    acc = (ub.astype(F32) * cw_ref[SSM_CONV - 1:SSM_CONV, :] + cb_ref[...]
           + jnp.dot(shift_ref[...], taps, preferred_element_type=F32))
    prev_ref[0:CONV_PAD, :] = ub[L - CONV_PAD:L, :]
    xc = acc * _sigmoid(acc)
    xs = xc[:, :SSM_INNER]
    bm = xc[:, SSM_INNER:SSM_INNER + SSM_GROUPS * N].astype(BF16)
    cm = xc[:, SSM_INNER + SSM_GROUPS * N:].astype(BF16)

    a_row = -jnp.exp(arow_ref[...])
    dt = _softplus(sm_ref[...] + dtbrow_ref[...])
    acum = _scan_rows(dt * a_row)
    a_col = -jnp.exp(acol_ref[...])
    dtt = _softplus(smt_ref[...] + dtbcol_ref[...])
    acumt = _scan_lanes(dtt * a_col)
    alast = acum[L - 1:L, :]
    ea = jnp.exp(acum)
    w2 = dt * jnp.exp(alast - acum)

    def expand(v):
        hi, lo = _split2(v)
        return jnp.dot(jnp.concatenate([hi, lo], axis=1), e2_ref[...], preferred_element_type=F32)

    dt_e = expand(dt)
    ea_e = expand(ea)
    w2_e = expand(w2)
    cd_e = ea_e[L - 1:L, :]

    xdt = (xs * dt_e).astype(BF16)
    xw2 = (xs * w2_e).astype(BF16)

    r2 = lax.broadcasted_iota(jnp.int32, (L, L), 0)
    c2 = lax.broadcasted_iota(jnp.int32, (L, L), 1)
    tri = r2 >= c2
    lane = lax.broadcasted_iota(jnp.int32, (L, LANES), 1)
    low = lane < SSM_HEAD_DIM

    ys = []
    for g in range(SSM_GROUPS):
        bg = bm[:, N * g:N * (g + 1)]
        cg = cm[:, N * g:N * (g + 1)]
        cb = lax.dot_general(cg, bg, (((1,), (1,)), ((), ())), preferred_element_type=F32)
        st_prev = state_ref[g]
        y_off = jnp.dot(cg, st_prev.astype(BF16), preferred_element_type=F32) * ea_e[:, GW * g:GW * (g + 1)]
        st_c = lax.dot_general(bg, xw2[:, GW * g:GW * (g + 1)], (((0,), (0,)), ((), ())),
                               preferred_element_type=F32)
        state_ref[g] = st_prev * cd_e[:, GW * g:GW * (g + 1)] + st_c
        yd = []
        for pj in range(GW // LANES):
            h0 = (GW // SSM_HEAD_DIM) * g + 2 * pj
            ms = []
            for h in (h0, h0 + 1):
                diff = acum[:, h:h + 1] - acumt[h:h + 1, :]
                ms.append((cb * jnp.exp(jnp.where(tri, diff, NEG_BIG))).astype(BF16))
            xp = xdt[:, LANES * (h0 // 2):LANES * (h0 // 2 + 1)]
            zero = jnp.zeros_like(xp)
            rhs = jnp.concatenate([jnp.where(low, xp, zero), jnp.where(low, zero, xp)], axis=0)
            yd.append(jnp.dot(jnp.concatenate(ms, axis=1), rhs, preferred_element_type=F32))
        ys.append(jnp.concatenate(yd, axis=1) + y_off)
    y = jnp.concatenate(ys, axis=1) + dexp_ref[...] * xs

    zz = z_ref[...]
    uu = y * (zz * _sigmoid(zz))
    outs = []
    for g in range(SSM_GROUPS):
        ug = uu[:, GW * g:GW * (g + 1)]
        ms_ = jnp.mean(ug * ug, axis=-1, keepdims=True)
        outs.append(ug * lax.rsqrt(ms_ + RMS_EPS))
    o_ref[...] = (jnp.concatenate(outs, axis=1) * nw_ref[...]).astype(o_ref.dtype)


def _ssd(z, xbc, sm, smt, conv_w, conv_b, dt_bias, a_log, d_skip, norm_w, bsz, seq):
    L = SSM_CHUNK
    nc = seq // L
    t = bsz * seq
    dtbrow = jnp.zeros((1, SMALL_W), F32).at[0, DT_OFF:DT_OFF + SSM_HEADS].set(dt_bias.astype(F32))
    arow = jnp.zeros((1, SMALL_W), F32).at[0, DT_OFF:DT_OFF + SSM_HEADS].set(a_log.astype(F32))
    dtbcol = jnp.zeros((SMALL_T, 1), F32).at[DT_OFF:DT_OFF + SSM_HEADS, 0].set(dt_bias.astype(F32))
    acol = jnp.zeros((SMALL_T, 1), F32).at[DT_OFF:DT_OFF + SSM_HEADS, 0].set(a_log.astype(F32))
    dtbcol = jnp.broadcast_to(dtbcol, (SMALL_T, L))
    acol = jnp.broadcast_to(acol, (SMALL_T, L))
    dexp = jnp.repeat(d_skip.astype(F32), SSM_HEAD_DIM)[None, :]
    e1 = np.zeros((SMALL_W, SSM_INNER), np.float32)
    for h in range(SSM_HEADS):
        e1[DT_OFF + h, SSM_HEAD_DIM * h:SSM_HEAD_DIM * (h + 1)] = 1.0
    e2 = jnp.asarray(np.concatenate([e1, e1], axis=0), BF16)
    win = CONV_PAD + L
    sh = np.zeros((L, (SSM_CONV - 1) * win), np.float32)
    for s in range(1, SSM_CONV):
        sh[np.arange(L), (s - 1) * win + CONV_PAD + np.arange(L) - s] = 1.0
    shift = jnp.asarray(sh, BF16)
    tok = lambda w: pl.BlockSpec((L, w), lambda b, c: (b * nc + c, 0))
    return pl.pallas_call(
        _ssd_kernel,
        grid=(bsz, nc),
        in_specs=[
            tok(SSM_INNER), tok(SSM_CONV_DIM), tok(SMALL_W),
            pl.BlockSpec((SMALL_T, L), lambda b, c: (0, b * nc + c)),
            _const_spec((SSM_CONV, SSM_CONV_DIM)), _const_spec((SSM_CONV, SSM_CONV_DIM)),
            _const_spec((1, SSM_CONV_DIM)), _const_spec(shift.shape),
            _const_spec(dtbrow.shape), _const_spec(dtbcol.shape), _const_spec(arow.shape), _const_spec(acol.shape),
            _const_spec(dexp.shape), _const_spec((1, SSM_INNER)), _const_spec(e2.shape),
        ],
        out_specs=tok(SSM_INNER),
        out_shape=jax.ShapeDtypeStruct((t, SSM_INNER), BF16),
        scratch_shapes=[pltpu.VMEM((CONV_PAD + L, SSM_CONV_DIM), BF16),
                        pltpu.VMEM((SSM_GROUPS, SSM_STATE, GROUP_WIDTH), F32)],
        compiler_params=_params("arbitrary", "arbitrary"),
        name="ssd",
    )(z, xbc, sm, smt, conv_w.astype(F32), conv_w.astype(BF16), conv_b.astype(F32)[None, :], shift,
      dtbrow, dtbcol, arow, acol, dexp, norm_w.astype(F32)[None, :], e2)


def _mix_kernel(attn_ref, ssm_ref, gl_ref, x_ref, wpa_ref, wps_ref, wout_ref, bg_ref, g_ref, b_ref, o_ref):
    ad = jnp.dot(attn_ref[...], wpa_ref[...], preferred_element_type=F32)
    sd = jnp.dot(ssm_ref[...], wps_ref[...], preferred_element_type=F32)
    gates = _sigmoid(gl_ref[...] + bg_ref[...])
    mix = gates[:, :D_MODEL] * ad + gates[:, D_MODEL:] * sd
    mixed = jnp.dot(mix.astype(BF16), wout_ref[...], preferred_element_type=F32)
    o_ref[...] = _layer_norm(DEEPNORM_ALPHA * x_ref[...] + mixed, g_ref[...], b_ref[...])


def _mix(attn, ssm, gl, x2, wpa, wps, wout, b_gates, ln_g, ln_b):
    t = x2.shape[0]
    tm = TM_MIX
    tok = lambda w: pl.BlockSpec((tm, w), lambda i: (i, 0))
    return pl.pallas_call(
        _mix_kernel,
        grid=(t // tm,),
        in_specs=[tok(ATT_WIDTH), tok(SSM_INNER), tok(2 * D_MODEL), tok(D_MODEL),
                  _const_spec(wpa.shape), _const_spec(wps.shape), _const_spec(wout.shape),
                  _const_spec((1, 2 * D_MODEL)), _const_spec((1, D_MODEL)), _const_spec((1, D_MODEL))],
        out_specs=tok(D_MODEL),
        out_shape=jax.ShapeDtypeStruct((t, D_MODEL), F32),
        compiler_params=_params("arbitrary"),
        name="mix",
    )(attn, ssm, gl, x2, wpa, wps, wout, b_gates.astype(F32)[None, :], ln_g.astype(F32)[None, :],
      ln_b.astype(F32)[None, :])


def _ffn_kernel(x_ref, wg_ref, wu_ref, wd_ref, g_ref, b_ref, o_ref):
    x = x_ref[...]
    xb = x.astype(BF16)
    hg = jnp.dot(xb, wg_ref[...], preferred_element_type=F32)
    hu = jnp.dot(xb, wu_ref[...], preferred_element_type=F32)
    h = (hg * _sigmoid(hg) * hu).astype(BF16)
    d = jnp.dot(h, wd_ref[...], preferred_element_type=F32)
    o_ref[...] = _layer_norm(DEEPNORM_ALPHA * x + d, g_ref[...], b_ref[...])


def _ffn(x1, wg, wu, wd, ln_g, ln_b):
    t = x1.shape[0]
    tm = TM_FFN
    tok = pl.BlockSpec((tm, D_MODEL), lambda i: (i, 0))
    return pl.pallas_call(
        _ffn_kernel,
        grid=(t // tm,),
        in_specs=[tok, _const_spec(wg.shape), _const_spec(wu.shape), _const_spec(wd.shape),
                  _const_spec((1, D_MODEL)), _const_spec((1, D_MODEL))],
        out_specs=tok,
        out_shape=jax.ShapeDtypeStruct((t, D_MODEL), F32),
        compiler_params=_params("arbitrary"),
        name="ffn",
    )(x1, wg, wu, wd, ln_g.astype(F32)[None, :], ln_b.astype(F32)[None, :])


def _pack_in_weights(w):
    o = IN_OFFS
    q, k, v, f = w[:, o[0]:o[1]], w[:, o[1]:o[2]], w[:, o[2]:o[3]], w[:, o[3]:o[4]]
    z, xbc, dt, gate = w[:, o[4]:o[5]], w[:, o[5]:o[6]], w[:, o[6]:o[7]], w[:, o[7]:o[8]]
    small = jnp.zeros((D_MODEL, SMALL_W), w.dtype)
    small = small.at[:, DT_OFF:DT_OFF + SSM_HEADS].set(dt).at[:, F_OFF:F_OFF + ATT_HEADS].set(f)
    wa = jnp.concatenate([k, z, xbc, gate, small], axis=1).astype(BF16)
    scale = LOG2E / math.sqrt(ATT_HEAD_DIM)
    wbt = jnp.concatenate([(q * scale).T, v.T, small[:, :SMALL_T].T], axis=0).astype(BF16)
    return wa, wbt


def kernel(x, w_in, b_forget, conv_w, conv_b, dt_bias, a_log, d_skip, ssm_norm_w, w_proj_attn,
           w_proj_ssm, b_gates, w_out, ln1_g, ln1_b, w_ffn_gate, w_ffn_up, w_ffn_down, ln2_g, ln2_b):
    bsz, seq, dm = x.shape
    assert dm == D_MODEL and seq % ATT_BLOCK == 0 and seq % SSM_CHUNK == 0
    assert w_in.shape[0] == DEPTH
    x2 = x.reshape(bsz * seq, dm)
    for l in range(DEPTH):
        wa, wbt = _pack_in_weights(w_in[l])
        k, z, xbc, gl, sm, qt, vt, smt = _inproj(x2, wa, wbt)
        ka, xq, base = _attnprep(sm, smt, k, b_forget[l], bsz, seq)
        attn = _attention(base, qt, xq, ka, vt, bsz, seq)
        ssm = _ssd(z, xbc, sm, smt, conv_w[l], conv_b[l], dt_bias[l], a_log[l], d_skip[l], ssm_norm_w[l],
                   bsz, seq)
        x1 = _mix(attn, ssm, gl, x2, w_proj_attn[l].astype(BF16), w_proj_ssm[l].astype(BF16),
                  w_out[l].astype(BF16), b_gates[l], ln1_g[l], ln1_b[l])
        x2 = _ffn(x1, w_ffn_gate[l].astype(BF16), w_ffn_up[l].astype(BF16), w_ffn_down[l].astype(BF16),
                  ln2_g[l], ln2_b[l])
    return x2.reshape(bsz, seq, dm)
```

```python
import functools
import math

import numpy as np
import jax
import jax.numpy as jnp
from jax import lax
from jax.experimental import pallas as pl
from jax.experimental.pallas import tpu as pltpu

F32 = jnp.float32
BF16 = jnp.bfloat16

D_MODEL = 1024
ATT_HEADS = 16
ATT_HEAD_DIM = 64
ATT_WIDTH = ATT_HEADS * ATT_HEAD_DIM
SSM_INNER = 2048
SSM_HEAD_DIM = 64
SSM_HEADS = 32
SSM_GROUPS = 4
SSM_STATE = 128
SSM_CONV = 4
SSM_CHUNK = 128
SSM_CONV_DIM = SSM_INNER + 2 * SSM_GROUPS * SSM_STATE
GROUP_WIDTH = SSM_INNER // SSM_GROUPS
FFN_HIDDEN = 2816
DEPTH = 1
DEEPNORM_ALPHA = (2 * DEPTH) ** 0.25
LN_EPS = 1e-5
RMS_EPS = 1e-5
IN_SIZES = (ATT_WIDTH, ATT_WIDTH, ATT_WIDTH, ATT_HEADS, SSM_INNER, SSM_CONV_DIM, SSM_HEADS, 2 * D_MODEL)
IN_OFFS = tuple(int(v) for v in np.concatenate([[0], np.cumsum(IN_SIZES)]))

LANES = 128
VMEM_LIMIT = 56 * 1024 * 1024

TM_INPROJ = 256
ATT_BLOCK = 512
ATT_QBLOCK = 1024
TM_MIX = 512
TM_FFN = 512
ROW_GROUPS = 2
SMALL_W = 128
SMALL_T = 64
DT_OFF = 0
F_OFF = 32
XQ_ROWS = 16
ACC_ROWS = ATT_HEAD_DIM + 16
ATT_ROW_CHUNK = 64
ATT_TRIP = 4
CONV_PAD = 16
LOG2E = math.log2(math.e)
NEG_BIG = -1e30


def _sigmoid(v):
    return 1.0 / (1.0 + jnp.exp(-v))


def _softplus(v):
    return jnp.maximum(v, 0.0) + jnp.log1p(jnp.exp(-jnp.abs(v)))


def _split2(v):
    hi = v.astype(BF16)
    lo = (v - hi.astype(F32)).astype(BF16)
    return hi, lo


def _split3(v):
    hi = v.astype(BF16)
    r = v - hi.astype(F32)
    lo = r.astype(BF16)
    lolo = (r - lo.astype(F32)).astype(BF16)
    return hi, lo, lolo


def _scan_rows(v):
    n = v.shape[0]
    row = lax.broadcasted_iota(jnp.int32, v.shape, 0)
    d = 1
    while d < n:
        v = v + jnp.where(row >= d, pltpu.roll(v, d, 0), 0.0)
        d *= 2
    return v


def _scan_lanes(v):
    n = v.shape[1]
    col = lax.broadcasted_iota(jnp.int32, v.shape, 1)
    d = 1
    while d < n:
        v = v + jnp.where(col >= d, pltpu.roll(v, d, 1), 0.0)
        d *= 2
    return v


def _layer_norm(y, g, b):
    mu = jnp.mean(y, axis=-1, keepdims=True)
    yc = y - mu
    var = jnp.mean(yc * yc, axis=-1, keepdims=True)
    return yc * lax.rsqrt(var + LN_EPS) * g + b


def _row_groups(n):
    g = n // ROW_GROUPS
    return [slice(k * g, (k + 1) * g) for k in range(ROW_GROUPS)]


def _const_spec(shape):
    nd = len(shape)
    return pl.BlockSpec(shape, lambda *_: (0,) * nd, pipeline_mode=pl.Buffered(1))


def _params(*sem):
    return pltpu.CompilerParams(dimension_semantics=sem, vmem_limit_bytes=VMEM_LIMIT)


WA_K = (0, 1024)
WA_Z = (1024, 3072)
WA_XBC = (3072, 6144)
WA_G = (6144, 8192)
WA_S = (8192, 8192 + SMALL_W)
WB_Q = (0, 1024)
WB_V = (1024, 2048)
WB_S = (2048, 2048 + SMALL_T)


def _inproj_kernel(x_ref, wa_ref, wbt_ref, k_ref, z_ref, xbc_ref, gl_ref, sm_ref, qt_ref, vt_ref, smt_ref):
    xb = x_ref[...].astype(BF16)

    def seg(r):
        return jnp.dot(xb, wa_ref[:, r[0]:r[1]], preferred_element_type=F32)

    def segt(r):
        return lax.dot_general(wbt_ref[r[0]:r[1], :], xb, (((1,), (1,)), ((), ())),
                               preferred_element_type=F32)

    k_ref[...] = seg(WA_K).astype(BF16)
    z_ref[...] = seg(WA_Z)
    xbc_ref[...] = seg(WA_XBC).astype(BF16)
    gl_ref[...] = seg(WA_G)
    sm_ref[...] = seg(WA_S)
    qt_ref[...] = segt(WB_Q).astype(BF16)
    vt_ref[...] = segt(WB_V).astype(BF16)
    smt_ref[...] = segt(WB_S)


def _inproj(x2, wa, wbt):
    t = x2.shape[0]
    tm = TM_INPROJ
    tok = lambda w: pl.BlockSpec((tm, w), lambda i: (i, 0))
    feat = lambda h: pl.BlockSpec((h, tm), lambda i: (0, i))
    return pl.pallas_call(
        _inproj_kernel,
        grid=(t // tm,),
        in_specs=[tok(D_MODEL), _const_spec(wa.shape), _const_spec(wbt.shape)],
        out_specs=[tok(1024), tok(2048), tok(3072), tok(2048), tok(SMALL_W),
                   feat(1024), feat(1024), feat(SMALL_T)],
        out_shape=[
            jax.ShapeDtypeStruct((t, 1024), BF16),
            jax.ShapeDtypeStruct((t, 2048), F32),
            jax.ShapeDtypeStruct((t, 3072), BF16),
            jax.ShapeDtypeStruct((t, 2048), F32),
            jax.ShapeDtypeStruct((t, SMALL_W), F32),
            jax.ShapeDtypeStruct((1024, t), BF16),
            jax.ShapeDtypeStruct((1024, t), BF16),
            jax.ShapeDtypeStruct((SMALL_T, t), F32),
        ],
        compiler_params=_params("arbitrary"),
        name="inproj",
    )(x2, wa, wbt)


def _attnprep_kernel(sm_ref, smt_ref, k_ref, bfrow_ref, bfcol_ref, shi_ref, slo_ref, sll_ref, crow_ref,
                     phi_ref, plo_ref, pll_ref, ka_ref, xq_ref, base_ref, carry_ref):
    i = pl.program_id(1)

    @pl.when(i == 0)
    def _():
        carry_ref[...] = jnp.zeros_like(carry_ref)

    g = sm_ref.shape[0]
    v = sm_ref[...] + bfrow_ref[...]
    lf = -_softplus(-v) * LOG2E
    rk = _scan_rows(lf)
    base_ref[0] = carry_ref[...]
    carry_ref[...] = carry_ref[...] + rk[g - 1:g, :]
    hi, lo, ll = _split3(rk)
    extras = (jnp.dot(hi, shi_ref[...], preferred_element_type=F32)
              + jnp.dot(lo, slo_ref[...], preferred_element_type=F32)
              + jnp.dot(ll, sll_ref[...], preferred_element_type=F32)
              + crow_ref[...])
    lane = lax.broadcasted_iota(jnp.int32, (g, LANES), 1)
    low = lane < ATT_HEAD_DIM
    for j in range(ATT_HEADS // 2):
        kk = k_ref[:, LANES * j:LANES * (j + 1)].astype(F32)
        ev = extras[:, 2 * LANES * j:2 * LANES * j + LANES]
        od = extras[:, 2 * LANES * j + LANES:2 * LANES * (j + 1)]
        ka_ref[:, 2 * LANES * j:2 * LANES * j + LANES] = jnp.where(low, kk, ev).astype(BF16)
        ka_ref[:, 2 * LANES * j + LANES:2 * LANES * (j + 1)] = jnp.where(low, od, kk).astype(BF16)

    vt = smt_ref[...] + bfcol_ref[...]
    lft = -_softplus(-vt) * LOG2E
    rq = _scan_lanes(lft)
    hq, lq, llq = _split3(rq)
    nrow = ATT_HEADS * XQ_ROWS
    row = lax.broadcasted_iota(jnp.int32, (nrow, g), 0)
    ones = jnp.where((row & (XQ_ROWS - 1)) < 3, 1.0, 0.0)
    xq = (jnp.dot(phi_ref[...], hq, preferred_element_type=F32)
          + jnp.dot(plo_ref[...], lq, preferred_element_type=F32)
          + jnp.dot(pll_ref[...], llq, preferred_element_type=F32)
          + ones)
    xq_ref[...] = xq.astype(BF16)


def _attnprep_consts():
    s_hi = np.zeros((SMALL_W, ATT_HEADS * LANES), np.float32)
    s_lo = np.zeros_like(s_hi)
    s_ll = np.zeros_like(s_hi)
    crow = np.zeros((1, ATT_HEADS * LANES), np.float32)
    p_hi = np.zeros((ATT_HEADS * XQ_ROWS, SMALL_T), np.float32)
    p_lo = np.zeros_like(p_hi)
    p_ll = np.zeros_like(p_hi)
    for h in range(ATT_HEADS):
        off = LANES * h + (ATT_HEAD_DIM if h % 2 == 0 else 0)
        s_hi[F_OFF + h, off + 0] = -1.0
        s_lo[F_OFF + h, off + 1] = -1.0
        s_ll[F_OFF + h, off + 2] = -1.0
        crow[0, off + 3:off + 6] = 1.0
        p_hi[XQ_ROWS * h + 3, F_OFF + h] = 1.0
        p_lo[XQ_ROWS * h + 4, F_OFF + h] = 1.0
        p_ll[XQ_ROWS * h + 5, F_OFF + h] = 1.0
    b = lambda a: jnp.asarray(a, BF16)
    return b(s_hi), b(s_lo), b(s_ll), jnp.asarray(crow), b(p_hi), b(p_lo), b(p_ll)


def _attnprep(sm, smt, k, b_forget, bsz, seq):
    g = ATT_BLOCK
    nb = seq // g
    s_hi, s_lo, s_ll, crow, p_hi, p_lo, p_ll = _attnprep_consts()
    bfrow = jnp.zeros((1, SMALL_W), F32).at[0, F_OFF:F_OFF + ATT_HEADS].set(b_forget.astype(F32))
    bfcol = jnp.zeros((SMALL_T, 1), F32).at[F_OFF:F_OFF + ATT_HEADS, 0].set(b_forget.astype(F32))
    bfcol = jnp.broadcast_to(bfcol, (SMALL_T, g))
    t = bsz * seq
    ka, xq, base = pl.pallas_call(
        _attnprep_kernel,
        grid=(bsz, nb),
        in_specs=[
            pl.BlockSpec((g, SMALL_W), lambda b, i: (b * nb + i, 0)),
            pl.BlockSpec((SMALL_T, g), lambda b, i: (0, b * nb + i)),
            pl.BlockSpec((g, ATT_WIDTH), lambda b, i: (b * nb + i, 0)),
            _const_spec(bfrow.shape), _const_spec(bfcol.shape),
            _const_spec(s_hi.shape), _const_spec(s_lo.shape), _const_spec(s_ll.shape), _const_spec(crow.shape),
            _const_spec(p_hi.shape), _const_spec(p_lo.shape), _const_spec(p_ll.shape),
        ],
        out_specs=[
            pl.BlockSpec((g, ATT_HEADS * LANES), lambda b, i: (b * nb + i, 0)),
            pl.BlockSpec((ATT_HEADS * XQ_ROWS, g), lambda b, i: (0, b * nb + i)),
            pl.BlockSpec((1, 1, SMALL_W), lambda b, i: (b * nb + i, 0, 0)),
        ],
        out_shape=[
            jax.ShapeDtypeStruct((t, ATT_HEADS * LANES), BF16),
            jax.ShapeDtypeStruct((ATT_HEADS * XQ_ROWS, t), BF16),
            jax.ShapeDtypeStruct((bsz * nb, 1, SMALL_W), F32),
        ],
        scratch_shapes=[pltpu.VMEM((1, SMALL_W), F32)],
        compiler_params=_params("arbitrary", "arbitrary"),
        name="attnprep",
    )(sm, smt, k, bfrow, bfcol, s_hi, s_lo, s_ll, crow, p_hi, p_lo, p_ll)
    base = base.reshape(bsz, nb, SMALL_W)[:, :, F_OFF:F_OFF + ATT_HEADS]
    base = jnp.transpose(base, (0, 2, 1)).reshape(-1)
    return ka, xq, base


def _attn_kernel(base_ref, qt_ref, xq_ref, ka_ref, vt_ref, o_ref, *scratch, nb):
    b = pl.program_id(0)
    pr = pl.program_id(1)
    i = pl.program_id(2)
    tq = qt_ref.shape[1]
    tk = ATT_BLOCK
    dh = ATT_HEAD_DIM
    s_slots = tuple(scratch[2 * d:2 * d + 2] for d in range(ATT_TRIP))
    p_refs = scratch[2 * ATT_TRIP:2 * ATT_TRIP + 2]
    acc_refs = scratch[2 * ATT_TRIP + 2:2 * ATT_TRIP + 4]
    m_refs = scratch[2 * ATT_TRIP + 4:2 * ATT_TRIP + 6]
    zpad = jnp.zeros((LANES - dh - XQ_ROWS, tq), BF16)
    qa = (jnp.concatenate([qt_ref[0:dh, :], xq_ref[0:XQ_ROWS, :], zpad], axis=0),
          jnp.concatenate([xq_ref[XQ_ROWS:2 * XQ_ROWS, :], zpad, qt_ref[dh:2 * dh, :]], axis=0))
    boff = [(b * ATT_HEADS + 2 * pr + e) * nb for e in range(2)]
    ones_rows = jnp.ones((ACC_ROWS - dh, tk), BF16)
    for e in range(2):
        acc_refs[e][...] = jnp.zeros_like(acc_refs[e])
        m_refs[e][...] = jnp.full(m_refs[e].shape, NEG_BIG, F32)
    qlane = lax.broadcasted_iota(jnp.int32, (1, tq), 1)
    qbase = [jnp.where(qlane >= tk, base_ref[boff[e] + 2 * i + 1], base_ref[boff[e] + 2 * i]) for e in range(2)]

    def scores(j, s_ref, e, lo):
        off = pl.multiple_of(j * tk, tk)
        kj = ka_ref[pl.ds(off, tk), LANES * e:LANES * (e + 1)]
        s = jnp.dot(kj, qa[e][:, lo:], preferred_element_type=F32)
        s_ref[:, lo:] = s
        return jnp.max(s, axis=0, keepdims=True)

    def accumulate(j, s_ref, mt, m, e, lo, masked):
        off = pl.multiple_of(j * tk, tk)
        w = tq - lo
        c = qbase[e][:, lo:] - base_ref[boff[e] + j]
        p_ref, acc_ref = p_refs[e], acc_refs[e]
        if masked:
            mt = None
            for r in range(tk // ATT_ROW_CHUNK):
                rows = slice(r * ATT_ROW_CHUNK, (r + 1) * ATT_ROW_CHUNK)
                krow = lax.broadcasted_iota(jnp.int32, (ATT_ROW_CHUNK, w), 0) + r * ATT_ROW_CHUNK
                qcol = lax.broadcasted_iota(jnp.int32, (ATT_ROW_CHUNK, w), 1)
                sc = jnp.where(krow <= qcol, s_ref[rows, lo:], NEG_BIG)
                s_ref[rows, lo:] = sc
                cm = jnp.max(sc, axis=0, keepdims=True)
                mt = cm if mt is None else jnp.maximum(mt, cm)
        m_old = m[:, lo:]
        m_new = jnp.maximum(m_old, mt + c)
        alpha = jnp.exp2(m_old - m_new)
        sh = m_new - c
        for r in range(tk // ATT_ROW_CHUNK):
            rows = slice(r * ATT_ROW_CHUNK, (r + 1) * ATT_ROW_CHUNK)
            p_ref[rows, lo:] = jnp.exp2(s_ref[rows, lo:] - sh).astype(BF16)
        vj = vt_ref[dh * e:dh * (e + 1), pl.ds(off, tk)]
        vaug = jnp.concatenate([vj, ones_rows], axis=0)
        acc_ref[:, lo:] = alpha * acc_ref[:, lo:] + jnp.dot(vaug, p_ref[:, lo:], preferred_element_type=F32)
        return m_new if lo == 0 else jnp.concatenate([m[:, :lo], m_new], axis=1)

    def trip(blocks):
        los = [0 if d is None else d * tk for _, d in blocks]
        mts = [[scores(j, s_slots[k][e], e, los[k]) for e in range(2)] for k, (j, _) in enumerate(blocks)]
        ms = [m_refs[e][0:1, :] for e in range(2)]
        for k, (j, d) in enumerate(blocks):
            ms = [accumulate(j, s_slots[k][e], mts[k][e], ms[e], e, los[k], d is not None) for e in range(2)]
        for e in range(2):
            m_refs[e][0:1, :] = ms[e]

    def full_trip(t, carry):
        trip([(ATT_TRIP * t + k, None) for k in range(ATT_TRIP)])
        return carry

    assert ATT_TRIP == 4
    lax.fori_loop(0, (2 * i) // ATT_TRIP, full_trip, 0)
    diag_blocks = [(2 * i, 0), (2 * i + 1, 1)]

    @pl.when(i % 2 == 0)
    def _():
        trip(diag_blocks)

    @pl.when(i % 2 == 1)
    def _():
        trip([(2 * i - 2, None), (2 * i - 1, None)] + diag_blocks)

    ot = jnp.concatenate([a_ref[0:dh, :] / a_ref[dh:dh + 1, :] for a_ref in acc_refs], axis=0)
    o_ref[...] = ot.T.astype(o_ref.dtype)


def _attention(base, qt, xq, ka, vt, bsz, seq):
    tq = ATT_QBLOCK
    tk = ATT_BLOCK
    assert tq == 2 * tk and seq % tq == 0
    nq = seq // tq
    nb = seq // tk
    npair = ATT_HEADS // 2
    t = bsz * seq
    return pl.pallas_call(
        functools.partial(_attn_kernel, nb=nb),
        grid=(bsz, npair, nq),
        in_specs=[
            pl.BlockSpec(memory_space=pltpu.SMEM),
            pl.BlockSpec((2 * ATT_HEAD_DIM, tq), lambda b, p, i: (p, b * nq + i)),
            pl.BlockSpec((2 * XQ_ROWS, tq), lambda b, p, i: (p, b * nq + i)),
            pl.BlockSpec((seq, 2 * LANES), lambda b, p, i: (b, p)),
            pl.BlockSpec((2 * ATT_HEAD_DIM, seq), lambda b, p, i: (p, b)),
        ],
        out_specs=pl.BlockSpec((tq, 2 * ATT_HEAD_DIM), lambda b, p, i: (b * nq + i, p)),
        out_shape=jax.ShapeDtypeStruct((t, ATT_WIDTH), BF16),
        scratch_shapes=([pltpu.VMEM((tk, tq), F32)] * (2 * ATT_TRIP)
                        + [pltpu.VMEM((tk, tq), BF16)] * 2
                        + [pltpu.VMEM((ACC_ROWS, tq), F32)] * 2
                        + [pltpu.VMEM((8, tq), F32)] * 2),
        compiler_params=_params("arbitrary", "arbitrary", "arbitrary"),
        name="attn",
    )(base, qt, xq, ka, vt)


def _ssd_kernel(z_ref, xbc_ref, sm_ref, smt_ref, cw_ref, cwb_ref, cb_ref, shift_ref, dtbrow_ref, dtbcol_ref,
                arow_ref, acol_ref, dexp_ref, nw_ref, e2_ref, o_ref, prev_ref, state_ref):
    c = pl.program_id(1)
    L = SSM_CHUNK
    N = SSM_STATE
    GW = GROUP_WIDTH

    @pl.when(c == 0)
    def _():
        prev_ref[0:CONV_PAD, :] = jnp.zeros((CONV_PAD, SSM_CONV_DIM), BF16)
        state_ref[...] = jnp.zeros_like(state_ref)

    ub = xbc_ref[...]
    prev_ref[CONV_PAD:CONV_PAD + L, :] = ub
    win = prev_ref[...]
    taps = jnp.concatenate([win * cwb_ref[SSM_CONV - 1 - s:SSM_CONV - s, :] for s in range(1, SSM_CONV)],
                           axis=0)
    acc = (ub.astype(F32) * cw_ref[SSM_CONV - 1:SSM_CONV, :] + cb_ref[...]
           + jnp.dot(shift_ref[...], taps, preferred_element_type=F32))
    prev_ref[0:CONV_PAD, :] = ub[L - CONV_PAD:L, :]
    xc = acc * _sigmoid(acc)
    xs = xc[:, :SSM_INNER]
    bm = xc[:, SSM_INNER:SSM_INNER + SSM_GROUPS * N].astype(BF16)
    cm = xc[:, SSM_INNER + SSM_GROUPS * N:].astype(BF16)

    a_row = -jnp.exp(arow_ref[...])
    dt = _softplus(sm_ref[...] + dtbrow_ref[...])
    acum = _scan_rows(dt * a_row)
    a_col = -jnp.exp(acol_ref[...])
    dtt = _softplus(smt_ref[...] + dtbcol_ref[...])
    acumt = _scan_lanes(dtt * a_col)
    alast = acum[L - 1:L, :]
    ea = jnp.exp(acum)
    w2 = dt * jnp.exp(alast - acum)

    def expand(v):
        hi, lo = _split2(v)
        return jnp.dot(jnp.concatenate([hi, lo], axis=1), e2_ref[...], preferred_element_type=F32)

    dt_e = expand(dt)
    ea_e = expand(ea)
    w2_e = expand(w2)
    cd_e = ea_e[L - 1:L, :]

    xdt = (xs * dt_e).astype(BF16)
    xw2 = (xs * w2_e).astype(BF16)

    r2 = lax.broadcasted_iota(jnp.int32, (L, L), 0)
    c2 = lax.broadcasted_iota(jnp.int32, (L, L), 1)
    tri = r2 >= c2
    lane = lax.broadcasted_iota(jnp.int32, (L, LANES), 1)
    low = lane < SSM_HEAD_DIM

    ys = []
    for g in range(SSM_GROUPS):
        bg = bm[:, N * g:N * (g + 1)]
        cg = cm[:, N * g:N * (g + 1)]
        cb = lax.dot_general(cg, bg, (((1,), (1,)), ((), ())), preferred_element_type=F32)
        st_prev = state_ref[g]
        y_off = jnp.dot(cg, st_prev.astype(BF16), preferred_element_type=F32) * ea_e[:, GW * g:GW * (g + 1)]
        st_c = lax.dot_general(bg, xw2[:, GW * g:GW * (g + 1)], (((0,), (0,)), ((), ())),
                               preferred_element_type=F32)
        state_ref[g] = st_prev * cd_e[:, GW * g:GW * (g + 1)] + st_c
        yd = []
        for pj in range(GW // LANES):
            h0 = (GW // SSM_HEAD_DIM) * g + 2 * pj
            ms = []
            for h in (h0, h0 + 1):
                diff = acum[:, h:h + 1] - acumt[h:h + 1, :]
                ms.append((cb * jnp.exp(jnp.where(tri, diff, NEG_BIG))).astype(BF16))
            xp = xdt[:, LANES * (h0 // 2):LANES * (h0 // 2 + 1)]
            zero = jnp.zeros_like(xp)
            rhs = jnp.concatenate([jnp.where(low, xp, zero), jnp.where(low, zero, xp)], axis=0)
            yd.append(jnp.dot(jnp.concatenate(ms, axis=1), rhs, preferred_element_type=F32))
        ys.append(jnp.concatenate(yd, axis=1) + y_off)
    y = jnp.concatenate(ys, axis=1) + dexp_ref[...] * xs

    zz = z_ref[...]
    uu = y * (zz * _sigmoid(zz))
    outs = []
    for g in range(SSM_GROUPS):
        ug = uu[:, GW * g:GW * (g + 1)]
        ms_ = jnp.mean(ug * ug, axis=-1, keepdims=True)
        outs.append(ug * lax.rsqrt(ms_ + RMS_EPS))
    o_ref[...] = (jnp.concatenate(outs, axis=1) * nw_ref[...]).astype(o_ref.dtype)


def _ssd(z, xbc, sm, smt, conv_w, conv_b, dt_bias, a_log, d_skip, norm_w, bsz, seq):
    L = SSM_CHUNK
    nc = seq // L
    t = bsz * seq
    dtbrow = jnp.zeros((1, SMALL_W), F32).at[0, DT_OFF:DT_OFF + SSM_HEADS].set(dt_bias.astype(F32))
    arow = jnp.zeros((1, SMALL_W), F32).at[0, DT_OFF:DT_OFF + SSM_HEADS].set(a_log.astype(F32))
    dtbcol = jnp.zeros((SMALL_T, 1), F32).at[DT_OFF:DT_OFF + SSM_HEADS, 0].set(dt_bias.astype(F32))
    acol = jnp.zeros((SMALL_T, 1), F32).at[DT_OFF:DT_OFF + SSM_HEADS, 0].set(a_log.astype(F32))
    dtbcol = jnp.broadcast_to(dtbcol, (SMALL_T, L))
    acol = jnp.broadcast_to(acol, (SMALL_T, L))
    dexp = jnp.repeat(d_skip.astype(F32), SSM_HEAD_DIM)[None, :]
    e1 = np.zeros((SMALL_W, SSM_INNER), np.float32)
    for h in range(SSM_HEADS):
        e1[DT_OFF + h, SSM_HEAD_DIM * h:SSM_HEAD_DIM * (h + 1)] = 1.0
    e2 = jnp.asarray(np.concatenate([e1, e1], axis=0), BF16)
    win = CONV_PAD + L
    sh = np.zeros((L, (SSM_CONV - 1) * win), np.float32)
    for s in range(1, SSM_CONV):
        sh[np.arange(L), (s - 1) * win + CONV_PAD + np.arange(L) - s] = 1.0
    shift = jnp.asarray(sh, BF16)
    tok = lambda w: pl.BlockSpec((L, w), lambda b, c: (b * nc + c, 0))
    return pl.pallas_call(
        _ssd_kernel,
        grid=(bsz, nc),
        in_specs=[
            tok(SSM_INNER), tok(SSM_CONV_DIM), tok(SMALL_W),
            pl.BlockSpec((SMALL_T, L), lambda b, c: (0, b * nc + c)),
            _const_spec((SSM_CONV, SSM_CONV_DIM)), _const_spec((SSM_CONV, SSM_CONV_DIM)),
            _const_spec((1, SSM_CONV_DIM)), _const_spec(shift.shape),
            _const_spec(dtbrow.shape), _const_spec(dtbcol.shape), _const_spec(arow.shape), _const_spec(acol.shape),
            _const_spec(dexp.shape), _const_spec((1, SSM_INNER)), _const_spec(e2.shape),
        ],
        out_specs=tok(SSM_INNER),
        out_shape=jax.ShapeDtypeStruct((t, SSM_INNER), BF16),
        scratch_shapes=[pltpu.VMEM((CONV_PAD + L, SSM_CONV_DIM), BF16),
                        pltpu.VMEM((SSM_GROUPS, SSM_STATE, GROUP_WIDTH), F32)],
        compiler_params=_params("arbitrary", "arbitrary"),
        name="ssd",
    )(z, xbc, sm, smt, conv_w.astype(F32), conv_w.astype(BF16), conv_b.astype(F32)[None, :], shift,
      dtbrow, dtbcol, arow, acol, dexp, norm_w.astype(F32)[None, :], e2)


def _mix_kernel(attn_ref, ssm_ref, gl_ref, x_ref, wpa_ref, wps_ref, wout_ref, bg_ref, g_ref, b_ref, o_ref):
    rows = _row_groups(x_ref.shape[0])
    ad = [jnp.dot(attn_ref[r, :], wpa_ref[...], preferred_element_type=F32) for r in rows]
    sd = [jnp.dot(ssm_ref[r, :], wps_ref[...], preferred_element_type=F32) for r in rows]
    for k, r in enumerate(rows):
        gates = _sigmoid(gl_ref[r, :] + bg_ref[...])
        mix = gates[:, :D_MODEL] * ad[k] + gates[:, D_MODEL:] * sd[k]
        mixed = jnp.dot(mix.astype(BF16), wout_ref[...], preferred_element_type=F32)
        o_ref[r, :] = _layer_norm(DEEPNORM_ALPHA * x_ref[r, :] + mixed, g_ref[...], b_ref[...])


def _mix(attn, ssm, gl, x2, wpa, wps, wout, b_gates, ln_g, ln_b):
    t = x2.shape[0]
    tm = TM_MIX
    tok = lambda w: pl.BlockSpec((tm, w), lambda i: (i, 0))
    return pl.pallas_call(
        _mix_kernel,
        grid=(t // tm,),
        in_specs=[tok(ATT_WIDTH), tok(SSM_INNER), tok(2 * D_MODEL), tok(D_MODEL),
                  _const_spec(wpa.shape), _const_spec(wps.shape), _const_spec(wout.shape),
                  _const_spec((1, 2 * D_MODEL)), _const_spec((1, D_MODEL)), _const_spec((1, D_MODEL))],
        out_specs=tok(D_MODEL),
        out_shape=jax.ShapeDtypeStruct((t, D_MODEL), F32),
        compiler_params=_params("arbitrary"),
        name="mix",
    )(attn, ssm, gl, x2, wpa, wps, wout, b_gates.astype(F32)[None, :], ln_g.astype(F32)[None, :],
      ln_b.astype(F32)[None, :])


def _ffn_kernel(x_ref, wg_ref, wu_ref, wd_ref, g_ref, b_ref, o_ref):
    rows = _row_groups(x_ref.shape[0])
    xb = [x_ref[r, :].astype(BF16) for r in rows]
    hg = [jnp.dot(v, wg_ref[...], preferred_element_type=F32) for v in xb]
    hu = [jnp.dot(v, wu_ref[...], preferred_element_type=F32) for v in xb]
    for k, r in enumerate(rows):
        h = (hg[k] * _sigmoid(hg[k]) * hu[k]).astype(BF16)
        d = jnp.dot(h, wd_ref[...], preferred_element_type=F32)
        o_ref[r, :] = _layer_norm(DEEPNORM_ALPHA * x_ref[r, :] + d, g_ref[...], b_ref[...])


def _ffn(x1, wg, wu, wd, ln_g, ln_b):
    t = x1.shape[0]
    tm = TM_FFN
    tok = pl.BlockSpec((tm, D_MODEL), lambda i: (i, 0))
    return pl.pallas_call(
        _ffn_kernel,
        grid=(t // tm,),
        in_specs=[tok, _const_spec(wg.shape), _const_spec(wu.shape), _const_spec(wd.shape),
                  _const_spec((1, D_MODEL)), _const_spec((1, D_MODEL))],
        out_specs=tok,
        out_shape=jax.ShapeDtypeStruct((t, D_MODEL), F32),
        compiler_params=_params("arbitrary"),
        name="ffn",
    )(x1, wg, wu, wd, ln_g.astype(F32)[None, :], ln_b.astype(F32)[None, :])


def _pack_in_weights(w):
    o = IN_OFFS
    q, k, v, f = w[:, o[0]:o[1]], w[:, o[1]:o[2]], w[:, o[2]:o[3]], w[:, o[3]:o[4]]
    z, xbc, dt, gate = w[:, o[4]:o[5]], w[:, o[5]:o[6]], w[:, o[6]:o[7]], w[:, o[7]:o[8]]
    assert DT_OFF == 0 and F_OFF == SSM_HEADS
    bf = lambda a: a.astype(BF16)
    small = jnp.concatenate([bf(dt), bf(f), jnp.zeros((D_MODEL, SMALL_W - SSM_HEADS - ATT_HEADS), BF16)], axis=1)
    wa = jnp.concatenate([bf(k), bf(z), bf(xbc), bf(gate), small], axis=1)
    scale = LOG2E / math.sqrt(ATT_HEAD_DIM)
    wbt = jnp.concatenate([bf(q * scale).T, bf(v).T, small[:, :SMALL_T].T], axis=0)
    return wa, wbt


def kernel(x, w_in, b_forget, conv_w, conv_b, dt_bias, a_log, d_skip, ssm_norm_w, w_proj_attn,
           w_proj_ssm, b_gates, w_out, ln1_g, ln1_b, w_ffn_gate, w_ffn_up, w_ffn_down, ln2_g, ln2_b):
    bsz, seq, dm = x.shape
    assert dm == D_MODEL and seq % ATT_BLOCK == 0 and seq % SSM_CHUNK == 0
    assert w_in.shape[0] == DEPTH
    x2 = x.reshape(bsz * seq, dm)
    for l in range(DEPTH):
        wa, wbt = _pack_in_weights(w_in[l])
        k, z, xbc, gl, sm, qt, vt, smt = _inproj(x2, wa, wbt)
        ka, xq, base = _attnprep(sm, smt, k, b_forget[l], bsz, seq)
        attn = _attention(base, qt, xq, ka, vt, bsz, seq)
        ssm = _ssd(z, xbc, sm, smt, conv_w[l], conv_b[l], dt_bias[l], a_log[l], d_skip[l], ssm_norm_w[l],
                   bsz, seq)
        x1 = _mix(attn, ssm, gl, x2, w_proj_attn[l].astype(BF16), w_proj_ssm[l].astype(BF16),
                  w_out[l].astype(BF16), b_gates[l], ln1_g[l], ln1_b[l])
        x2 = _ffn(x1, w_ffn_gate[l].astype(BF16), w_ffn_up[l].astype(BF16), w_ffn_down[l].astype(BF16),
                  ln2_g[l], ln2_b[l])
    return x2.reshape(bsz, seq, dm)
```

```python
import functools
import math

import numpy as np
import jax
import jax.numpy as jnp
from jax import lax
from jax.experimental import pallas as pl
from jax.experimental.pallas import tpu as pltpu

F32 = jnp.float32
BF16 = jnp.bfloat16

D_MODEL = 1024
ATT_HEADS = 16
ATT_HEAD_DIM = 64
ATT_WIDTH = ATT_HEADS * ATT_HEAD_DIM
SSM_INNER = 2048
SSM_HEAD_DIM = 64
SSM_HEADS = 32
SSM_GROUPS = 4
SSM_STATE = 128
SSM_CONV = 4
SSM_CHUNK = 128
SSM_CONV_DIM = SSM_INNER + 2 * SSM_GROUPS * SSM_STATE
GROUP_WIDTH = SSM_INNER // SSM_GROUPS
FFN_HIDDEN = 2816
DEPTH = 1
DEEPNORM_ALPHA = (2 * DEPTH) ** 0.25
LN_EPS = 1e-5
RMS_EPS = 1e-5
IN_SIZES = (ATT_WIDTH, ATT_WIDTH, ATT_WIDTH, ATT_HEADS, SSM_INNER, SSM_CONV_DIM, SSM_HEADS, 2 * D_MODEL)
IN_OFFS = tuple(int(v) for v in np.concatenate([[0], np.cumsum(IN_SIZES)]))

LANES = 128
VMEM_LIMIT = 56 * 1024 * 1024

TM_INPROJ = 256
ATT_BLOCK = 512
ATT_QBLOCK = 1024
TM_MIX = 512
TM_FFN = 512
ROW_GROUPS = 2
SMALL_W = 128
SMALL_T = 64
DT_OFF = 0
F_OFF = 32
XQ_ROWS = 16
ACC_ROWS = ATT_HEAD_DIM + 16
ATT_ROW_CHUNK = 64
ATT_TRIP = 4
CONV_PAD = 16
LOG2E = math.log2(math.e)
NEG_BIG = -1e30


def _sigmoid(v):
    return 1.0 / (1.0 + jnp.exp(-v))


def _softplus(v):
    return jnp.maximum(v, 0.0) + jnp.log1p(jnp.exp(-jnp.abs(v)))


def _split2(v):
    hi = v.astype(BF16)
    lo = (v - hi.astype(F32)).astype(BF16)
    return hi, lo


def _split3(v):
    hi = v.astype(BF16)
    r = v - hi.astype(F32)
    lo = r.astype(BF16)
    lolo = (r - lo.astype(F32)).astype(BF16)
    return hi, lo, lolo


def _scan_rows(v):
    n = v.shape[0]
    row = lax.broadcasted_iota(jnp.int32, v.shape, 0)
    d = 1
    while d < n:
        v = v + jnp.where(row >= d, pltpu.roll(v, d, 0), 0.0)
        d *= 2
    return v


def _scan_lanes(v):
    n = v.shape[1]
    col = lax.broadcasted_iota(jnp.int32, v.shape, 1)
    d = 1
    while d < n:
        v = v + jnp.where(col >= d, pltpu.roll(v, d, 1), 0.0)
        d *= 2
    return v


def _layer_norm(y, g, b):
    mu = jnp.mean(y, axis=-1, keepdims=True)
    yc = y - mu
    var = jnp.mean(yc * yc, axis=-1, keepdims=True)
    return yc * lax.rsqrt(var + LN_EPS) * g + b


def _row_groups(n):
    g = n // ROW_GROUPS
    return [slice(k * g, (k + 1) * g) for k in range(ROW_GROUPS)]


def _const_spec(shape):
    nd = len(shape)
    return pl.BlockSpec(shape, lambda *_: (0,) * nd, pipeline_mode=pl.Buffered(1))


def _params(*sem):
    return pltpu.CompilerParams(dimension_semantics=sem, vmem_limit_bytes=VMEM_LIMIT)


WA_Q = (0, 1024)
WA_K = (1024, 2048)
WA_V = (2048, 3072)
WA_Z = (3072, 5120)
WA_XBC = (5120, 8192)
WA_G = (8192, 10240)
WA_S = (10240, 10240 + SMALL_W)


def _inproj_kernel(x_ref, wa_ref, k_ref, z_ref, xbc_ref, gl_ref, sm_ref, qt_ref, vt_ref, smt_ref):
    xb = x_ref[...].astype(BF16)

    def seg(r):
        return jnp.dot(xb, wa_ref[:, r[0]:r[1]], preferred_element_type=F32)

    k_ref[...] = seg(WA_K).astype(BF16)
    z_ref[...] = seg(WA_Z)
    xbc_ref[...] = seg(WA_XBC).astype(BF16)
    gl_ref[...] = seg(WA_G)
    sm = seg(WA_S)
    sm_ref[...] = sm
    qt_ref[...] = seg(WA_Q).T.astype(BF16)
    vt_ref[...] = seg(WA_V).T.astype(BF16)
    smt_ref[...] = sm.T[0:SMALL_T, :]


def _inproj(x2, wa):
    t = x2.shape[0]
    tm = TM_INPROJ
    tok = lambda w: pl.BlockSpec((tm, w), lambda i: (i, 0))
    feat = lambda h: pl.BlockSpec((h, tm), lambda i: (0, i))
    return pl.pallas_call(
        _inproj_kernel,
        grid=(t // tm,),
        in_specs=[tok(D_MODEL), _const_spec(wa.shape)],
        out_specs=[tok(1024), tok(2048), tok(3072), tok(2048), tok(SMALL_W),
                   feat(1024), feat(1024), feat(SMALL_T)],
        out_shape=[
            jax.ShapeDtypeStruct((t, 1024), BF16),
            jax.ShapeDtypeStruct((t, 2048), F32),
            jax.ShapeDtypeStruct((t, 3072), BF16),
            jax.ShapeDtypeStruct((t, 2048), F32),
            jax.ShapeDtypeStruct((t, SMALL_W), F32),
            jax.ShapeDtypeStruct((1024, t), BF16),
            jax.ShapeDtypeStruct((1024, t), BF16),
            jax.ShapeDtypeStruct((SMALL_T, t), F32),
        ],
        compiler_params=_params("arbitrary"),
        name="inproj",
    )(x2, wa)


def _attnprep_kernel(sm_ref, smt_ref, k_ref, bfrow_ref, bfcol_ref, shi_ref, crow_ref,
                     phi_ref, plo_ref, pll_ref, ka_ref, xq_ref, base_ref, carry_ref):
    i = pl.program_id(1)

    @pl.when(i == 0)
    def _():
        carry_ref[...] = jnp.zeros_like(carry_ref)

    g = sm_ref.shape[0]
    v = sm_ref[...] + bfrow_ref[...]
    lf = -_softplus(-v) * LOG2E
    rk = _scan_rows(lf)
    base_ref[0] = carry_ref[...]
    carry_ref[...] = carry_ref[...] + rk[g - 1:g, :]
    lane = lax.broadcasted_iota(jnp.int32, (g, LANES), 1)
    low = lane < ATT_HEAD_DIM
    heads = (lane >= F_OFF) & (lane < F_OFF + ATT_HEADS)
    hi, lo, ll = (jnp.where(heads, part.astype(F32), 0.0) for part in _split3(rk))
    packed = hi + pltpu.roll(lo, ATT_HEADS, 1) + pltpu.roll(ll, 2 * ATT_HEADS, 1)
    extras = jnp.dot(packed.astype(BF16), shi_ref[...], preferred_element_type=F32) + crow_ref[...]
    for j in range(ATT_HEADS // 2):
        kk = k_ref[:, LANES * j:LANES * (j + 1)].astype(F32)
        ev = extras[:, 2 * LANES * j:2 * LANES * j + LANES]
        od = extras[:, 2 * LANES * j + LANES:2 * LANES * (j + 1)]
        ka_ref[:, 2 * LANES * j:2 * LANES * j + LANES] = jnp.where(low, kk, ev).astype(BF16)
        ka_ref[:, 2 * LANES * j + LANES:2 * LANES * (j + 1)] = jnp.where(low, od, kk).astype(BF16)

    vt = smt_ref[...] + bfcol_ref[...]
    lft = -_softplus(-vt) * LOG2E
    rq = _scan_lanes(lft)
    hq, lq, llq = _split3(rq)
    nrow = ATT_HEADS * XQ_ROWS
    row = lax.broadcasted_iota(jnp.int32, (nrow, g), 0)
    ones = jnp.where((row & (XQ_ROWS - 1)) < 3, 1.0, 0.0)
    xq = (jnp.dot(phi_ref[...], hq, preferred_element_type=F32)
          + jnp.dot(plo_ref[...], lq, preferred_element_type=F32)
          + jnp.dot(pll_ref[...], llq, preferred_element_type=F32)
          + ones)
    xq_ref[...] = xq.astype(BF16)


def _attnprep_consts():
    assert F_OFF + 3 * ATT_HEADS <= SMALL_W
    s_hi = np.zeros((SMALL_W, ATT_HEADS * LANES), np.float32)
    crow = np.zeros((1, ATT_HEADS * LANES), np.float32)
    p_hi = np.zeros((ATT_HEADS * XQ_ROWS, SMALL_T), np.float32)
    p_lo = np.zeros_like(p_hi)
    p_ll = np.zeros_like(p_hi)
    for h in range(ATT_HEADS):
        off = LANES * h + (ATT_HEAD_DIM if h % 2 == 0 else 0)
        for part in range(3):
            s_hi[F_OFF + part * ATT_HEADS + h, off + part] = -1.0
        crow[0, off + 3:off + 6] = 1.0
        p_hi[XQ_ROWS * h + 3, F_OFF + h] = 1.0
        p_lo[XQ_ROWS * h + 4, F_OFF + h] = 1.0
        p_ll[XQ_ROWS * h + 5, F_OFF + h] = 1.0
    b = lambda a: jnp.asarray(a, BF16)
    return b(s_hi), jnp.asarray(crow), b(p_hi), b(p_lo), b(p_ll)


def _attnprep(sm, smt, k, b_forget, bsz, seq):
    g = ATT_BLOCK
    nb = seq // g
    s_hi, crow, p_hi, p_lo, p_ll = _attnprep_consts()
    bfrow = jnp.zeros((1, SMALL_W), F32).at[0, F_OFF:F_OFF + ATT_HEADS].set(b_forget.astype(F32))
    bfcol = jnp.zeros((SMALL_T, 1), F32).at[F_OFF:F_OFF + ATT_HEADS, 0].set(b_forget.astype(F32))
    bfcol = jnp.broadcast_to(bfcol, (SMALL_T, g))
    t = bsz * seq
    ka, xq, base = pl.pallas_call(
        _attnprep_kernel,
        grid=(bsz, nb),
        in_specs=[
            pl.BlockSpec((g, SMALL_W), lambda b, i: (b * nb + i, 0)),
            pl.BlockSpec((SMALL_T, g), lambda b, i: (0, b * nb + i)),
            pl.BlockSpec((g, ATT_WIDTH), lambda b, i: (b * nb + i, 0)),
            _const_spec(bfrow.shape), _const_spec(bfcol.shape),
            _const_spec(s_hi.shape), _const_spec(crow.shape),
            _const_spec(p_hi.shape), _const_spec(p_lo.shape), _const_spec(p_ll.shape),
        ],
        out_specs=[
            pl.BlockSpec((g, ATT_HEADS * LANES), lambda b, i: (b * nb + i, 0)),
            pl.BlockSpec((ATT_HEADS * XQ_ROWS, g), lambda b, i: (0, b * nb + i)),
            pl.BlockSpec((1, 1, SMALL_W), lambda b, i: (b * nb + i, 0, 0)),
        ],
        out_shape=[
            jax.ShapeDtypeStruct((t, ATT_HEADS * LANES), BF16),
            jax.ShapeDtypeStruct((ATT_HEADS * XQ_ROWS, t), BF16),
            jax.ShapeDtypeStruct((bsz * nb, 1, SMALL_W), F32),
        ],
        scratch_shapes=[pltpu.VMEM((1, SMALL_W), F32)],
        compiler_params=_params("arbitrary", "arbitrary"),
        name="attnprep",
    )(sm, smt, k, bfrow, bfcol, s_hi, crow, p_hi, p_lo, p_ll)
    base = base.reshape(bsz, nb, SMALL_W)[:, :, F_OFF:F_OFF + ATT_HEADS]
    base = jnp.transpose(base, (0, 2, 1)).reshape(-1)
    return ka, xq, base


def _attn_kernel(base_ref, qt_ref, xq_ref, ka_ref, vt_ref, o_ref, *scratch, nb):
    b = pl.program_id(0)
    pr = pl.program_id(1)
    i = pl.program_id(2)
    tq = qt_ref.shape[1]
    tk = ATT_BLOCK
    dh = ATT_HEAD_DIM
    s_slots = tuple(scratch[2 * d:2 * d + 2] for d in range(ATT_TRIP))
    p_refs = scratch[2 * ATT_TRIP:2 * ATT_TRIP + 2]
    acc_refs = scratch[2 * ATT_TRIP + 2:2 * ATT_TRIP + 4]
    m_refs = scratch[2 * ATT_TRIP + 4:2 * ATT_TRIP + 6]
    zpad = jnp.zeros((LANES - dh - XQ_ROWS, tq), BF16)
    qa = (jnp.concatenate([qt_ref[0:dh, :], xq_ref[0:XQ_ROWS, :], zpad], axis=0),
          jnp.concatenate([xq_ref[XQ_ROWS:2 * XQ_ROWS, :], zpad, qt_ref[dh:2 * dh, :]], axis=0))
    boff = [(b * ATT_HEADS + 2 * pr + e) * nb for e in range(2)]
    ones_rows = jnp.ones((ACC_ROWS - dh, tk), BF16)
    for e in range(2):
        acc_refs[e][...] = jnp.zeros_like(acc_refs[e])
        m_refs[e][...] = jnp.full(m_refs[e].shape, NEG_BIG, F32)
    qlane = lax.broadcasted_iota(jnp.int32, (1, tq), 1)
    qbase = [jnp.where(qlane >= tk, base_ref[boff[e] + 2 * i + 1], base_ref[boff[e] + 2 * i]) for e in range(2)]

    def scores(j, s_ref, e, lo):
        off = pl.multiple_of(j * tk, tk)
        kj = ka_ref[pl.ds(off, tk), LANES * e:LANES * (e + 1)]
        s = jnp.dot(kj, qa[e][:, lo:], preferred_element_type=F32)
        s_ref[:, lo:] = s
        return jnp.max(s, axis=0, keepdims=True)

    def accumulate(j, s_ref, mt, m, e, lo, masked):
        off = pl.multiple_of(j * tk, tk)
        w = tq - lo
        c = qbase[e][:, lo:] - base_ref[boff[e] + j]
        p_ref, acc_ref = p_refs[e], acc_refs[e]
        if masked:
            mt = None
            for r in range(tk // ATT_ROW_CHUNK):
                rows = slice(r * ATT_ROW_CHUNK, (r + 1) * ATT_ROW_CHUNK)
                krow = lax.broadcasted_iota(jnp.int32, (ATT_ROW_CHUNK, w), 0) + r * ATT_ROW_CHUNK
                qcol = lax.broadcasted_iota(jnp.int32, (ATT_ROW_CHUNK, w), 1)
                sc = jnp.where(krow <= qcol, s_ref[rows, lo:], NEG_BIG)
                s_ref[rows, lo:] = sc
                cm = jnp.max(sc, axis=0, keepdims=True)
                mt = cm if mt is None else jnp.maximum(mt, cm)
        m_old = m[:, lo:]
        m_new = jnp.maximum(m_old, mt + c)
        alpha = jnp.exp2(m_old - m_new)
        sh = m_new - c
        for r in range(tk // ATT_ROW_CHUNK):
            rows = slice(r * ATT_ROW_CHUNK, (r + 1) * ATT_ROW_CHUNK)
            p_ref[rows, lo:] = jnp.exp2(s_ref[rows, lo:] - sh).astype(BF16)
        vj = vt_ref[dh * e:dh * (e + 1), pl.ds(off, tk)]
        vaug = jnp.concatenate([vj, ones_rows], axis=0)
        acc_ref[:, lo:] = alpha * acc_ref[:, lo:] + jnp.dot(vaug, p_ref[:, lo:], preferred_element_type=F32)
        return m_new if lo == 0 else jnp.concatenate([m[:, :lo], m_new], axis=1)

    def trip(blocks):
        los = [0 if d is None else d * tk for _, d in blocks]
        mts = [[scores(j, s_slots[k][e], e, los[k]) for e in range(2)] for k, (j, _) in enumerate(blocks)]
        ms = [m_refs[e][0:1, :] for e in range(2)]
        for k, (j, d) in enumerate(blocks):
            ms = [accumulate(j, s_slots[k][e], mts[k][e], ms[e], e, los[k], d is not None) for e in range(2)]
        for e in range(2):
            m_refs[e][0:1, :] = ms[e]

    def full_trip(t, carry):
        trip([(ATT_TRIP * t + k, None) for k in range(ATT_TRIP)])
        return carry

    assert ATT_TRIP == 4
    lax.fori_loop(0, (2 * i) // ATT_TRIP, full_trip, 0)
    diag_blocks = [(2 * i, 0), (2 * i + 1, 1)]

    @pl.when(i % 2 == 0)
    def _():
        trip(diag_blocks)

    @pl.when(i % 2 == 1)
    def _():
        trip([(2 * i - 2, None), (2 * i - 1, None)] + diag_blocks)

    ot = jnp.concatenate([a_ref[0:dh, :] / a_ref[dh:dh + 1, :] for a_ref in acc_refs], axis=0)
    o_ref[...] = ot.T.astype(o_ref.dtype)


def _attention(base, qt, xq, ka, vt, bsz, seq):
    tq = ATT_QBLOCK
    tk = ATT_BLOCK
    assert tq == 2 * tk and seq % tq == 0
    nq = seq // tq
    nb = seq // tk
    npair = ATT_HEADS // 2
    t = bsz * seq
    return pl.pallas_call(
        functools.partial(_attn_kernel, nb=nb),
        grid=(bsz, npair, nq),
        in_specs=[
            pl.BlockSpec(memory_space=pltpu.SMEM),
            pl.BlockSpec((2 * ATT_HEAD_DIM, tq), lambda b, p, i: (p, b * nq + i)),
            pl.BlockSpec((2 * XQ_ROWS, tq), lambda b, p, i: (p, b * nq + i)),
            pl.BlockSpec((seq, 2 * LANES), lambda b, p, i: (b, p)),
            pl.BlockSpec((2 * ATT_HEAD_DIM, seq), lambda b, p, i: (p, b)),
        ],
        out_specs=pl.BlockSpec((tq, 2 * ATT_HEAD_DIM), lambda b, p, i: (b * nq + i, p)),
        out_shape=jax.ShapeDtypeStruct((t, ATT_WIDTH), BF16),
        scratch_shapes=([pltpu.VMEM((tk, tq), F32)] * (2 * ATT_TRIP)
                        + [pltpu.VMEM((tk, tq), BF16)] * 2
                        + [pltpu.VMEM((ACC_ROWS, tq), F32)] * 2
                        + [pltpu.VMEM((8, tq), F32)] * 2),
        compiler_params=_params("arbitrary", "arbitrary", "arbitrary"),
        name="attn",
    )(base, qt, xq, ka, vt)


def _ssd_kernel(z_ref, xbc_ref, sm_ref, smt_ref, cw_ref, cwb_ref, cb_ref, shift_ref, dtbrow_ref, dtbcol_ref,
                arow_ref, acol_ref, dexp_ref, nw_ref, e2_ref, o_ref, prev_ref, state_ref):
    c = pl.program_id(1)
    L = SSM_CHUNK
    N = SSM_STATE
    GW = GROUP_WIDTH

    @pl.when(c == 0)
    def _():
        prev_ref[0:CONV_PAD, :] = jnp.zeros((CONV_PAD, SSM_CONV_DIM), BF16)
        state_ref[...] = jnp.zeros_like(state_ref)

    ub = xbc_ref[...]
    prev_ref[CONV_PAD:CONV_PAD + L, :] = ub
    win = prev_ref[...]
    taps = jnp.concatenate([win * cwb_ref[SSM_CONV - 1 - s:SSM_CONV - s, :] for s in range(1, SSM_CONV)],
                           axis=0)
    acc = (ub.astype(F32) * cw_ref[SSM_CONV - 1:SSM_CONV, :] + cb_ref[...]
           + jnp.dot(shift_ref[...], taps, preferred_element_type=F32))
    prev_ref[0:CONV_PAD, :] = ub[L - CONV_PAD:L, :]
    xc = acc * _sigmoid(acc)
    xs = xc[:, :SSM_INNER]
    bm = xc[:, SSM_INNER:SSM_INNER + SSM_GROUPS * N].astype(BF16)
    cm = xc[:, SSM_INNER + SSM_GROUPS * N:].astype(BF16)

    a_row = -jnp.exp(arow_ref[...])
    dt = _softplus(sm_ref[...] + dtbrow_ref[...])
    acum = _scan_rows(dt * a_row)
    a_col = -jnp.exp(acol_ref[...])
    dtt = _softplus(smt_ref[...] + dtbcol_ref[...])
    acumt = _scan_lanes(dtt * a_col)
    alast = acum[L - 1:L, :]
    ea = jnp.exp(acum)
    w2 = dt * jnp.exp(alast - acum)

    def expand(v):
        hi, lo = _split2(v)
        return jnp.dot(jnp.concatenate([hi, lo], axis=1), e2_ref[...], preferred_element_type=F32)

    dt_e = expand(dt)
    ea_e = expand(ea)
    w2_e = expand(w2)
    cd_e = ea_e[L - 1:L, :]

    xdt = (xs * dt_e).astype(BF16)
    xw2 = (xs * w2_e).astype(BF16)

    r2 = lax.broadcasted_iota(jnp.int32, (L, L), 0)
    c2 = lax.broadcasted_iota(jnp.int32, (L, L), 1)
    tri = r2 >= c2
    lane = lax.broadcasted_iota(jnp.int32, (L, LANES), 1)
    low = lane < SSM_HEAD_DIM

    ys = []
    for g in range(SSM_GROUPS):
        bg = bm[:, N * g:N * (g + 1)]
        cg = cm[:, N * g:N * (g + 1)]
        cb = lax.dot_general(cg, bg, (((1,), (1,)), ((), ())), preferred_element_type=F32)
        st_prev = state_ref[g]
        y_off = jnp.dot(cg, st_prev.astype(BF16), preferred_element_type=F32) * ea_e[:, GW * g:GW * (g + 1)]
        st_c = lax.dot_general(bg, xw2[:, GW * g:GW * (g + 1)], (((0,), (0,)), ((), ())),
                               preferred_element_type=F32)
        state_ref[g] = st_prev * cd_e[:, GW * g:GW * (g + 1)] + st_c
        yd = []
        for pj in range(GW // LANES):
            h0 = (GW // SSM_HEAD_DIM) * g + 2 * pj
            ms = []
            for h in (h0, h0 + 1):
                diff = acum[:, h:h + 1] - acumt[h:h + 1, :]
                ms.append((cb * jnp.exp(jnp.where(tri, diff, NEG_BIG))).astype(BF16))
            xp = xdt[:, LANES * (h0 // 2):LANES * (h0 // 2 + 1)]
            zero = jnp.zeros_like(xp)
            rhs = jnp.concatenate([jnp.where(low, xp, zero), jnp.where(low, zero, xp)], axis=0)
            yd.append(jnp.dot(jnp.concatenate(ms, axis=1), rhs, preferred_element_type=F32))
        ys.append(jnp.concatenate(yd, axis=1) + y_off)
    y = jnp.concatenate(ys, axis=1) + dexp_ref[...] * xs

    zz = z_ref[...]
    uu = y * (zz * _sigmoid(zz))
    outs = []
    for g in range(SSM_GROUPS):
        ug = uu[:, GW * g:GW * (g + 1)]
        ms_ = jnp.mean(ug * ug, axis=-1, keepdims=True)
        outs.append(ug * lax.rsqrt(ms_ + RMS_EPS))
    o_ref[...] = (jnp.concatenate(outs, axis=1) * nw_ref[...]).astype(o_ref.dtype)


def _ssd(z, xbc, sm, smt, conv_w, conv_b, dt_bias, a_log, d_skip, norm_w, bsz, seq):
    L = SSM_CHUNK
    nc = seq // L
    t = bsz * seq
    dtbrow = jnp.zeros((1, SMALL_W), F32).at[0, DT_OFF:DT_OFF + SSM_HEADS].set(dt_bias.astype(F32))
    arow = jnp.zeros((1, SMALL_W), F32).at[0, DT_OFF:DT_OFF + SSM_HEADS].set(a_log.astype(F32))
    dtbcol = jnp.zeros((SMALL_T, 1), F32).at[DT_OFF:DT_OFF + SSM_HEADS, 0].set(dt_bias.astype(F32))
    acol = jnp.zeros((SMALL_T, 1), F32).at[DT_OFF:DT_OFF + SSM_HEADS, 0].set(a_log.astype(F32))
    dtbcol = jnp.broadcast_to(dtbcol, (SMALL_T, L))
    acol = jnp.broadcast_to(acol, (SMALL_T, L))
    dexp = jnp.repeat(d_skip.astype(F32), SSM_HEAD_DIM)[None, :]
    e1 = np.zeros((SMALL_W, SSM_INNER), np.float32)
    for h in range(SSM_HEADS):
        e1[DT_OFF + h, SSM_HEAD_DIM * h:SSM_HEAD_DIM * (h + 1)] = 1.0
    e2 = jnp.asarray(np.concatenate([e1, e1], axis=0), BF16)
    win = CONV_PAD + L
    sh = np.zeros((L, (SSM_CONV - 1) * win), np.float32)
    for s in range(1, SSM_CONV):
        sh[np.arange(L), (s - 1) * win + CONV_PAD + np.arange(L) - s] = 1.0
    shift = jnp.asarray(sh, BF16)
    tok = lambda w: pl.BlockSpec((L, w), lambda b, c: (b * nc + c, 0))
    return pl.pallas_call(
        _ssd_kernel,
        grid=(bsz, nc),
        in_specs=[
            tok(SSM_INNER), tok(SSM_CONV_DIM), tok(SMALL_W),
            pl.BlockSpec((SMALL_T, L), lambda b, c: (0, b * nc + c)),
            _const_spec((SSM_CONV, SSM_CONV_DIM)), _const_spec((SSM_CONV, SSM_CONV_DIM)),
            _const_spec((1, SSM_CONV_DIM)), _const_spec(shift.shape),
            _const_spec(dtbrow.shape), _const_spec(dtbcol.shape), _const_spec(arow.shape), _const_spec(acol.shape),
            _const_spec(dexp.shape), _const_spec((1, SSM_INNER)), _const_spec(e2.shape),
        ],
        out_specs=tok(SSM_INNER),
        out_shape=jax.ShapeDtypeStruct((t, SSM_INNER), BF16),
        scratch_shapes=[pltpu.VMEM((CONV_PAD + L, SSM_CONV_DIM), BF16),
                        pltpu.VMEM((SSM_GROUPS, SSM_STATE, GROUP_WIDTH), F32)],
        compiler_params=_params("arbitrary", "arbitrary"),
        name="ssd",
    )(z, xbc, sm, smt, conv_w.astype(F32), conv_w.astype(BF16), conv_b.astype(F32)[None, :], shift,
      dtbrow, dtbcol, arow, acol, dexp, norm_w.astype(F32)[None, :], e2)


def _mix_kernel(attn_ref, ssm_ref, gl_ref, x_ref, wpa_ref, wps_ref, wout_ref, bg_ref, g_ref, b_ref, o_ref):
    rows = _row_groups(x_ref.shape[0])
    ad = [jnp.dot(attn_ref[r, :], wpa_ref[...], preferred_element_type=F32) for r in rows]
    sd = [jnp.dot(ssm_ref[r, :], wps_ref[...], preferred_element_type=F32) for r in rows]
    for k, r in enumerate(rows):
        gates = _sigmoid(gl_ref[r, :] + bg_ref[...])
        mix = gates[:, :D_MODEL] * ad[k] + gates[:, D_MODEL:] * sd[k]
        mixed = jnp.dot(mix.astype(BF16), wout_ref[...], preferred_element_type=F32)
        o_ref[r, :] = _layer_norm(DEEPNORM_ALPHA * x_ref[r, :] + mixed, g_ref[...], b_ref[...])


def _mix(attn, ssm, gl, x2, wpa, wps, wout, b_gates, ln_g, ln_b):
    t = x2.shape[0]
    tm = TM_MIX
    tok = lambda w: pl.BlockSpec((tm, w), lambda i: (i, 0))
    return pl.pallas_call(
        _mix_kernel,
        grid=(t // tm,),
        in_specs=[tok(ATT_WIDTH), tok(SSM_INNER), tok(2 * D_MODEL), tok(D_MODEL),
                  _const_spec(wpa.shape), _const_spec(wps.shape), _const_spec(wout.shape),
                  _const_spec((1, 2 * D_MODEL)), _const_spec((1, D_MODEL)), _const_spec((1, D_MODEL))],
        out_specs=tok(D_MODEL),
        out_shape=jax.ShapeDtypeStruct((t, D_MODEL), F32),
        compiler_params=_params("arbitrary"),
        name="mix",
    )(attn, ssm, gl, x2, wpa, wps, wout, b_gates.astype(F32)[None, :], ln_g.astype(F32)[None, :],
      ln_b.astype(F32)[None, :])


def _ffn_kernel(x_ref, wg_ref, wu_ref, wd_ref, g_ref, b_ref, o_ref):
    rows = _row_groups(x_ref.shape[0])
    xb = [x_ref[r, :].astype(BF16) for r in rows]
    hg = [jnp.dot(v, wg_ref[...], preferred_element_type=F32) for v in xb]
    hu = [jnp.dot(v, wu_ref[...], preferred_element_type=F32) for v in xb]
    for k, r in enumerate(rows):
        h = (hg[k] * _sigmoid(hg[k]) * hu[k]).astype(BF16)
        d = jnp.dot(h, wd_ref[...], preferred_element_type=F32)
        o_ref[r, :] = _layer_norm(DEEPNORM_ALPHA * x_ref[r, :] + d, g_ref[...], b_ref[...])


def _ffn(x1, wg, wu, wd, ln_g, ln_b):
    t = x1.shape[0]
    tm = TM_FFN
    tok = pl.BlockSpec((tm, D_MODEL), lambda i: (i, 0))
    return pl.pallas_call(
        _ffn_kernel,
        grid=(t // tm,),
        in_specs=[tok, _const_spec(wg.shape), _const_spec(wu.shape), _const_spec(wd.shape),
                  _const_spec((1, D_MODEL)), _const_spec((1, D_MODEL))],
        out_specs=tok,
        out_shape=jax.ShapeDtypeStruct((t, D_MODEL), F32),
        compiler_params=_params("arbitrary"),
        name="ffn",
    )(x1, wg, wu, wd, ln_g.astype(F32)[None, :], ln_b.astype(F32)[None, :])


def _pack_in_weights(w):
    o = IN_OFFS
    q, k, v, f = w[:, o[0]:o[1]], w[:, o[1]:o[2]], w[:, o[2]:o[3]], w[:, o[3]:o[4]]
    z, xbc, dt, gate = w[:, o[4]:o[5]], w[:, o[5]:o[6]], w[:, o[6]:o[7]], w[:, o[7]:o[8]]
    assert DT_OFF == 0 and F_OFF == SSM_HEADS
    bf = lambda a: a.astype(BF16)
    small = jnp.concatenate([bf(dt), bf(f), jnp.zeros((D_MODEL, SMALL_W - SSM_HEADS - ATT_HEADS), BF16)], axis=1)
    scale = LOG2E / math.sqrt(ATT_HEAD_DIM)
    return jnp.concatenate([bf(q * scale), bf(k), bf(v), bf(z), bf(xbc), bf(gate), small], axis=1)


def kernel(x, w_in, b_forget, conv_w, conv_b, dt_bias, a_log, d_skip, ssm_norm_w, w_proj_attn,
           w_proj_ssm, b_gates, w_out, ln1_g, ln1_b, w_ffn_gate, w_ffn_up, w_ffn_down, ln2_g, ln2_b):
    bsz, seq, dm = x.shape
    assert dm == D_MODEL and seq % ATT_BLOCK == 0 and seq % SSM_CHUNK == 0
    assert w_in.shape[0] == DEPTH
    x2 = x.reshape(bsz * seq, dm)
    for l in range(DEPTH):
        k, z, xbc, gl, sm, qt, vt, smt = _inproj(x2, _pack_in_weights(w_in[l]))
        ka, xq, base = _attnprep(sm, smt, k, b_forget[l], bsz, seq)
        attn = _attention(base, qt, xq, ka, vt, bsz, seq)
        ssm = _ssd(z, xbc, sm, smt, conv_w[l], conv_b[l], dt_bias[l], a_log[l], d_skip[l], ssm_norm_w[l],
                   bsz, seq)
        x1 = _mix(attn, ssm, gl, x2, w_proj_attn[l].astype(BF16), w_proj_ssm[l].astype(BF16),
                  w_out[l].astype(BF16), b_gates[l], ln1_g[l], ln1_b[l])
        x2 = _ffn(x1, w_ffn_gate[l].astype(BF16), w_ffn_up[l].astype(BF16), w_ffn_down[l].astype(BF16),
                  ln2_g[l], ln2_b[l])
    return x2.reshape(bsz, seq, dm)
```

```python
import functools
import math

import numpy as np
import jax
import jax.numpy as jnp
from jax import lax
from jax.experimental import pallas as pl
from jax.experimental.pallas import tpu as pltpu

F32 = jnp.float32
BF16 = jnp.bfloat16

D_MODEL = 1024
ATT_HEADS = 16
ATT_HEAD_DIM = 64
ATT_WIDTH = ATT_HEADS * ATT_HEAD_DIM
SSM_INNER = 2048
SSM_HEAD_DIM = 64
SSM_HEADS = 32
SSM_GROUPS = 4
SSM_STATE = 128
SSM_CONV = 4
SSM_CHUNK = 128
SSM_CONV_DIM = SSM_INNER + 2 * SSM_GROUPS * SSM_STATE
GROUP_WIDTH = SSM_INNER // SSM_GROUPS
FFN_HIDDEN = 2816
DEPTH = 1
DEEPNORM_ALPHA = (2 * DEPTH) ** 0.25
LN_EPS = 1e-5
RMS_EPS = 1e-5
IN_SIZES = (ATT_WIDTH, ATT_WIDTH, ATT_WIDTH, ATT_HEADS, SSM_INNER, SSM_CONV_DIM, SSM_HEADS, 2 * D_MODEL)
IN_OFFS = tuple(int(v) for v in np.concatenate([[0], np.cumsum(IN_SIZES)]))

LANES = 128
VMEM_LIMIT = 56 * 1024 * 1024

TM_INPROJ = 256
ATT_BLOCK = 512
ATT_QBLOCK = 1024
TM_MIX = 512
TM_FFN = 512
ROW_GROUPS = 2
SMALL_W = 128
SMALL_T = 64
DT_OFF = 0
F_OFF = 32
XQ_ROWS = 16
ACC_ROWS = ATT_HEAD_DIM + 16
ATT_ROW_CHUNK = 64
ATT_TRIP = 4
CONV_PAD = 16
LOG2E = math.log2(math.e)
NEG_BIG = -1e30


def _sigmoid(v):
    return 1.0 / (1.0 + jnp.exp(-v))


def _softplus(v):
    return jnp.maximum(v, 0.0) + jnp.log1p(jnp.exp(-jnp.abs(v)))


def _split2(v):
    hi = v.astype(BF16)
    lo = (v - hi.astype(F32)).astype(BF16)
    return hi, lo


def _split3(v):
    hi = v.astype(BF16)
    r = v - hi.astype(F32)
    lo = r.astype(BF16)
    lolo = (r - lo.astype(F32)).astype(BF16)
    return hi, lo, lolo


def _scan_rows(v):
    n = v.shape[0]
    row = lax.broadcasted_iota(jnp.int32, v.shape, 0)
    d = 1
    while d < n:
        v = v + jnp.where(row >= d, pltpu.roll(v, d, 0), 0.0)
        d *= 2
    return v


def _scan_lanes(v):
    n = v.shape[1]
    col = lax.broadcasted_iota(jnp.int32, v.shape, 1)
    d = 1
    while d < n:
        v = v + jnp.where(col >= d, pltpu.roll(v, d, 1), 0.0)
        d *= 2
    return v


def _layer_norm(y, g, b):
    mu = jnp.mean(y, axis=-1, keepdims=True)
    yc = y - mu
    var = jnp.mean(yc * yc, axis=-1, keepdims=True)
    return yc * lax.rsqrt(var + LN_EPS) * g + b


def _row_groups(n):
    g = n // ROW_GROUPS
    return [slice(k * g, (k + 1) * g) for k in range(ROW_GROUPS)]


def _const_spec(shape):
    nd = len(shape)
    return pl.BlockSpec(shape, lambda *_: (0,) * nd, pipeline_mode=pl.Buffered(1))


def _params(*sem):
    return pltpu.CompilerParams(dimension_semantics=sem, vmem_limit_bytes=VMEM_LIMIT)


def _inproj_kernel(x_ref, wq_ref, wk_ref, wv_ref, wz_ref, wxbc_ref, wg_ref, ws_ref,
                   k_ref, z_ref, xbc_ref, gl_ref, sm_ref, qt_ref, vt_ref, smt_ref):
    xb = x_ref[...].astype(BF16)

    def proj(w_ref):
        return jnp.dot(xb, w_ref[...], preferred_element_type=F32)

    k_ref[...] = proj(wk_ref).astype(BF16)
    z_ref[...] = proj(wz_ref)
    xbc_ref[...] = proj(wxbc_ref).astype(BF16)
    gl_ref[...] = proj(wg_ref)
    sm = proj(ws_ref)
    sm_ref[...] = sm
    qt_ref[...] = proj(wq_ref).T.astype(BF16)
    vt_ref[...] = proj(wv_ref).T.astype(BF16)
    smt_ref[...] = sm.T[0:SMALL_T, :]


def _inproj(x2, weights):
    t = x2.shape[0]
    tm = TM_INPROJ
    tok = lambda w: pl.BlockSpec((tm, w), lambda i: (i, 0))
    feat = lambda h: pl.BlockSpec((h, tm), lambda i: (0, i))
    return pl.pallas_call(
        _inproj_kernel,
        grid=(t // tm,),
        in_specs=[tok(D_MODEL)] + [_const_spec(w.shape) for w in weights],
        out_specs=[tok(1024), tok(2048), tok(3072), tok(2048), tok(SMALL_W),
                   feat(1024), feat(1024), feat(SMALL_T)],
        out_shape=[
            jax.ShapeDtypeStruct((t, 1024), BF16),
            jax.ShapeDtypeStruct((t, 2048), F32),
            jax.ShapeDtypeStruct((t, 3072), BF16),
            jax.ShapeDtypeStruct((t, 2048), F32),
            jax.ShapeDtypeStruct((t, SMALL_W), F32),
            jax.ShapeDtypeStruct((1024, t), BF16),
            jax.ShapeDtypeStruct((1024, t), BF16),
            jax.ShapeDtypeStruct((SMALL_T, t), F32),
        ],
        compiler_params=_params("arbitrary"),
        name="inproj",
    )(x2, *weights)


def _attnprep_kernel(sm_ref, smt_ref, k_ref, bfrow_ref, bfcol_ref, shi_ref, crow_ref,
                     phi_ref, plo_ref, pll_ref, ka_ref, xq_ref, base_ref, carry_ref):
    i = pl.program_id(1)

    @pl.when(i == 0)
    def _():
        carry_ref[...] = jnp.zeros_like(carry_ref)

    g = sm_ref.shape[0]
    v = sm_ref[...] + bfrow_ref[...]
    lf = -_softplus(-v) * LOG2E
    rk = _scan_rows(lf)
    base_ref[0] = carry_ref[...]
    carry_ref[...] = carry_ref[...] + rk[g - 1:g, :]
    lane = lax.broadcasted_iota(jnp.int32, (g, LANES), 1)
    low = lane < ATT_HEAD_DIM
    heads = (lane >= F_OFF) & (lane < F_OFF + ATT_HEADS)
    hi, lo, ll = (jnp.where(heads, part.astype(F32), 0.0) for part in _split3(rk))
    packed = hi + pltpu.roll(lo, ATT_HEADS, 1) + pltpu.roll(ll, 2 * ATT_HEADS, 1)
    extras = jnp.dot(packed.astype(BF16), shi_ref[...], preferred_element_type=F32) + crow_ref[...]
    for j in range(ATT_HEADS // 2):
        kk = k_ref[:, LANES * j:LANES * (j + 1)].astype(F32)
        ev = extras[:, 2 * LANES * j:2 * LANES * j + LANES]
        od = extras[:, 2 * LANES * j + LANES:2 * LANES * (j + 1)]
        ka_ref[:, 2 * LANES * j:2 * LANES * j + LANES] = jnp.where(low, kk, ev).astype(BF16)
        ka_ref[:, 2 * LANES * j + LANES:2 * LANES * (j + 1)] = jnp.where(low, od, kk).astype(BF16)

    vt = smt_ref[...] + bfcol_ref[...]
    lft = -_softplus(-vt) * LOG2E
    rq = _scan_lanes(lft)
    hq, lq, llq = _split3(rq)
    nrow = ATT_HEADS * XQ_ROWS
    row = lax.broadcasted_iota(jnp.int32, (nrow, g), 0)
    ones = jnp.where((row & (XQ_ROWS - 1)) < 3, 1.0, 0.0)
    xq = (jnp.dot(phi_ref[...], hq, preferred_element_type=F32)
          + jnp.dot(plo_ref[...], lq, preferred_element_type=F32)
          + jnp.dot(pll_ref[...], llq, preferred_element_type=F32)
          + ones)
    xq_ref[...] = xq.astype(BF16)


def _attnprep_consts():
    assert F_OFF + 3 * ATT_HEADS <= SMALL_W
    s_hi = np.zeros((SMALL_W, ATT_HEADS * LANES), np.float32)
    crow = np.zeros((1, ATT_HEADS * LANES), np.float32)
    p_hi = np.zeros((ATT_HEADS * XQ_ROWS, SMALL_T), np.float32)
    p_lo = np.zeros_like(p_hi)
    p_ll = np.zeros_like(p_hi)
    for h in range(ATT_HEADS):
        off = LANES * h + (ATT_HEAD_DIM if h % 2 == 0 else 0)
        for part in range(3):
            s_hi[F_OFF + part * ATT_HEADS + h, off + part] = -1.0
        crow[0, off + 3:off + 6] = 1.0
        p_hi[XQ_ROWS * h + 3, F_OFF + h] = 1.0
        p_lo[XQ_ROWS * h + 4, F_OFF + h] = 1.0
        p_ll[XQ_ROWS * h + 5, F_OFF + h] = 1.0
    b = lambda a: jnp.asarray(a, BF16)
    return b(s_hi), jnp.asarray(crow), b(p_hi), b(p_lo), b(p_ll)


def _attnprep(sm, smt, k, b_forget, bsz, seq):
    g = ATT_BLOCK
    nb = seq // g
    s_hi, crow, p_hi, p_lo, p_ll = _attnprep_consts()
    bfrow = jnp.zeros((1, SMALL_W), F32).at[0, F_OFF:F_OFF + ATT_HEADS].set(b_forget.astype(F32))
    bfcol = jnp.zeros((SMALL_T, 1), F32).at[F_OFF:F_OFF + ATT_HEADS, 0].set(b_forget.astype(F32))
    bfcol = jnp.broadcast_to(bfcol, (SMALL_T, g))
    t = bsz * seq
    ka, xq, base = pl.pallas_call(
        _attnprep_kernel,
        grid=(bsz, nb),
        in_specs=[
            pl.BlockSpec((g, SMALL_W), lambda b, i: (b * nb + i, 0)),
            pl.BlockSpec((SMALL_T, g), lambda b, i: (0, b * nb + i)),
            pl.BlockSpec((g, ATT_WIDTH), lambda b, i: (b * nb + i, 0)),
            _const_spec(bfrow.shape), _const_spec(bfcol.shape),
            _const_spec(s_hi.shape), _const_spec(crow.shape),
            _const_spec(p_hi.shape), _const_spec(p_lo.shape), _const_spec(p_ll.shape),
        ],
        out_specs=[
            pl.BlockSpec((g, ATT_HEADS * LANES), lambda b, i: (b * nb + i, 0)),
            pl.BlockSpec((ATT_HEADS * XQ_ROWS, g), lambda b, i: (0, b * nb + i)),
            pl.BlockSpec((1, 1, SMALL_W), lambda b, i: (b * nb + i, 0, 0)),
        ],
        out_shape=[
            jax.ShapeDtypeStruct((t, ATT_HEADS * LANES), BF16),
            jax.ShapeDtypeStruct((ATT_HEADS * XQ_ROWS, t), BF16),
            jax.ShapeDtypeStruct((bsz * nb, 1, SMALL_W), F32),
        ],
        scratch_shapes=[pltpu.VMEM((1, SMALL_W), F32)],
        compiler_params=_params("arbitrary", "arbitrary"),
        name="attnprep",
    )(sm, smt, k, bfrow, bfcol, s_hi, crow, p_hi, p_lo, p_ll)
    base = base.reshape(bsz, nb, SMALL_W)[:, :, F_OFF:F_OFF + ATT_HEADS]
    base = jnp.transpose(base, (0, 2, 1)).reshape(-1)
    return ka, xq, base


def _attn_kernel(base_ref, qt_ref, xq_ref, ka_ref, vt_ref, o_ref, *scratch, nb):
    b = pl.program_id(0)
    pr = pl.program_id(1)
    i = pl.program_id(2)
    tq = qt_ref.shape[1]
    tk = ATT_BLOCK
    dh = ATT_HEAD_DIM
    s_slots = tuple(scratch[2 * d:2 * d + 2] for d in range(ATT_TRIP))
    p_refs = scratch[2 * ATT_TRIP:2 * ATT_TRIP + 2]
    acc_refs = scratch[2 * ATT_TRIP + 2:2 * ATT_TRIP + 4]
    m_refs = scratch[2 * ATT_TRIP + 4:2 * ATT_TRIP + 6]
    zpad = jnp.zeros((LANES - dh - XQ_ROWS, tq), BF16)
    qa = (jnp.concatenate([qt_ref[0:dh, :], xq_ref[0:XQ_ROWS, :], zpad], axis=0),
          jnp.concatenate([xq_ref[XQ_ROWS:2 * XQ_ROWS, :], zpad, qt_ref[dh:2 * dh, :]], axis=0))
    boff = [(b * ATT_HEADS + 2 * pr + e) * nb for e in range(2)]
    ones_rows = jnp.ones((ACC_ROWS - dh, tk), BF16)
    for e in range(2):
        acc_refs[e][...] = jnp.zeros_like(acc_refs[e])
        m_refs[e][...] = jnp.full(m_refs[e].shape, NEG_BIG, F32)
    qlane = lax.broadcasted_iota(jnp.int32, (1, tq), 1)
    qbase = [jnp.where(qlane >= tk, base_ref[boff[e] + 2 * i + 1], base_ref[boff[e] + 2 * i]) for e in range(2)]

    def scores(j, s_ref, e, lo):
        off = pl.multiple_of(j * tk, tk)
        kj = ka_ref[pl.ds(off, tk), LANES * e:LANES * (e + 1)]
        s = jnp.dot(kj, qa[e][:, lo:], preferred_element_type=F32)
        s_ref[:, lo:] = s
        return jnp.max(s, axis=0, keepdims=True)

    def accumulate(j, s_ref, mt, m, e, lo, masked):
        off = pl.multiple_of(j * tk, tk)
        w = tq - lo
        c = qbase[e][:, lo:] - base_ref[boff[e] + j]
        p_ref, acc_ref = p_refs[e], acc_refs[e]
        if masked:
            mt = None
            for r in range(tk // ATT_ROW_CHUNK):
                rows = slice(r * ATT_ROW_CHUNK, (r + 1) * ATT_ROW_CHUNK)
                krow = lax.broadcasted_iota(jnp.int32, (ATT_ROW_CHUNK, w), 0) + r * ATT_ROW_CHUNK
                qcol = lax.broadcasted_iota(jnp.int32, (ATT_ROW_CHUNK, w), 1)
                sc = jnp.where(krow <= qcol, s_ref[rows, lo:], NEG_BIG)
                s_ref[rows, lo:] = sc
                cm = jnp.max(sc, axis=0, keepdims=True)
                mt = cm if mt is None else jnp.maximum(mt, cm)
        m_old = m[:, lo:]
        m_new = jnp.maximum(m_old, mt + c)
        alpha = jnp.exp2(m_old - m_new)
        sh = m_new - c
        for r in range(tk // ATT_ROW_CHUNK):
            rows = slice(r * ATT_ROW_CHUNK, (r + 1) * ATT_ROW_CHUNK)
            p_ref[rows, lo:] = jnp.exp2(s_ref[rows, lo:] - sh).astype(BF16)
        vj = vt_ref[dh * e:dh * (e + 1), pl.ds(off, tk)]
        vaug = jnp.concatenate([vj, ones_rows], axis=0)
        acc_ref[:, lo:] = alpha * acc_ref[:, lo:] + jnp.dot(vaug, p_ref[:, lo:], preferred_element_type=F32)
        return m_new if lo == 0 else jnp.concatenate([m[:, :lo], m_new], axis=1)

    def trip(blocks):
        los = [0 if d is None else d * tk for _, d in blocks]
        mts = [[scores(j, s_slots[k][e], e, los[k]) for e in range(2)] for k, (j, _) in enumerate(blocks)]
        ms = [m_refs[e][0:1, :] for e in range(2)]
        for k, (j, d) in enumerate(blocks):
            ms = [accumulate(j, s_slots[k][e], mts[k][e], ms[e], e, los[k], d is not None) for e in range(2)]
        for e in range(2):
            m_refs[e][0:1, :] = ms[e]

    def full_trip(t, carry):
        trip([(ATT_TRIP * t + k, None) for k in range(ATT_TRIP)])
        return carry

    assert ATT_TRIP == 4
    lax.fori_loop(0, (2 * i) // ATT_TRIP, full_trip, 0)
    diag_blocks = [(2 * i, 0), (2 * i + 1, 1)]

    @pl.when(i % 2 == 0)
    def _():
        trip(diag_blocks)

    @pl.when(i % 2 == 1)
    def _():
        trip([(2 * i - 2, None), (2 * i - 1, None)] + diag_blocks)

    ot = jnp.concatenate([a_ref[0:dh, :] / a_ref[dh:dh + 1, :] for a_ref in acc_refs], axis=0)
    o_ref[...] = ot.T.astype(o_ref.dtype)


def _attention(base, qt, xq, ka, vt, bsz, seq):
    tq = ATT_QBLOCK
    tk = ATT_BLOCK
    assert tq == 2 * tk and seq % tq == 0
    nq = seq // tq
    nb = seq // tk
    npair = ATT_HEADS // 2
    t = bsz * seq
    return pl.pallas_call(
        functools.partial(_attn_kernel, nb=nb),
        grid=(bsz, npair, nq),
        in_specs=[
            pl.BlockSpec(memory_space=pltpu.SMEM),
            pl.BlockSpec((2 * ATT_HEAD_DIM, tq), lambda b, p, i: (p, b * nq + i)),
            pl.BlockSpec((2 * XQ_ROWS, tq), lambda b, p, i: (p, b * nq + i)),
            pl.BlockSpec((seq, 2 * LANES), lambda b, p, i: (b, p)),
            pl.BlockSpec((2 * ATT_HEAD_DIM, seq), lambda b, p, i: (p, b)),
        ],
        out_specs=pl.BlockSpec((tq, 2 * ATT_HEAD_DIM), lambda b, p, i: (b * nq + i, p)),
        out_shape=jax.ShapeDtypeStruct((t, ATT_WIDTH), BF16),
        scratch_shapes=([pltpu.VMEM((tk, tq), F32)] * (2 * ATT_TRIP)
                        + [pltpu.VMEM((tk, tq), BF16)] * 2
                        + [pltpu.VMEM((ACC_ROWS, tq), F32)] * 2
                        + [pltpu.VMEM((8, tq), F32)] * 2),
        compiler_params=_params("arbitrary", "arbitrary", "arbitrary"),
        name="attn",
    )(base, qt, xq, ka, vt)


def _ssd_kernel(z_ref, xbc_ref, sm_ref, smt_ref, cw_ref, cwb_ref, cb_ref, shift_ref, dtbrow_ref, dtbcol_ref,
                arow_ref, acol_ref, dexp_ref, nw_ref, e2_ref, o_ref, prev_ref, state_ref):
    c = pl.program_id(1)
    L = SSM_CHUNK
    N = SSM_STATE
    GW = GROUP_WIDTH

    @pl.when(c == 0)
    def _():
        prev_ref[0:CONV_PAD, :] = jnp.zeros((CONV_PAD, SSM_CONV_DIM), BF16)
        state_ref[...] = jnp.zeros_like(state_ref)

    ub = xbc_ref[...]
    prev_ref[CONV_PAD:CONV_PAD + L, :] = ub
    win = prev_ref[...]
    taps = jnp.concatenate([win * cwb_ref[SSM_CONV - 1 - s:SSM_CONV - s, :] for s in range(1, SSM_CONV)],
                           axis=0)
    acc = (ub.astype(F32) * cw_ref[SSM_CONV - 1:SSM_CONV, :] + cb_ref[...]
           + jnp.dot(shift_ref[...], taps, preferred_element_type=F32))
    prev_ref[0:CONV_PAD, :] = ub[L - CONV_PAD:L, :]
    xc = acc * _sigmoid(acc)
    xs = xc[:, :SSM_INNER]
    bm = xc[:, SSM_INNER:SSM_INNER + SSM_GROUPS * N].astype(BF16)
    cm = xc[:, SSM_INNER + SSM_GROUPS * N:].astype(BF16)

    a_row = -jnp.exp(arow_ref[...])
    dt = _softplus(sm_ref[...] + dtbrow_ref[...])
    acum = _scan_rows(dt * a_row)
    a_col = -jnp.exp(acol_ref[...])
    dtt = _softplus(smt_ref[...] + dtbcol_ref[...])
    acumt = _scan_lanes(dtt * a_col)
    alast = acum[L - 1:L, :]
    ea = jnp.exp(acum)
    w2 = dt * jnp.exp(alast - acum)

    def expand(v):
        hi, lo = _split2(v)
        return jnp.dot(jnp.concatenate([hi, lo], axis=1), e2_ref[...], preferred_element_type=F32)

    dt_e = expand(dt)
    ea_e = expand(ea)
    w2_e = expand(w2)
    cd_e = ea_e[L - 1:L, :]

    xdt = (xs * dt_e).astype(BF16)
    xw2 = (xs * w2_e).astype(BF16)

    r2 = lax.broadcasted_iota(jnp.int32, (L, L), 0)
    c2 = lax.broadcasted_iota(jnp.int32, (L, L), 1)
    tri = r2 >= c2
    lane = lax.broadcasted_iota(jnp.int32, (L, LANES), 1)
    low = lane < SSM_HEAD_DIM

    ys = []
    for g in range(SSM_GROUPS):
        bg = bm[:, N * g:N * (g + 1)]
        cg = cm[:, N * g:N * (g + 1)]
        cb = lax.dot_general(cg, bg, (((1,), (1,)), ((), ())), preferred_element_type=F32)
        st_prev = state_ref[g]
        y_off = jnp.dot(cg, st_prev.astype(BF16), preferred_element_type=F32) * ea_e[:, GW * g:GW * (g + 1)]
        st_c = lax.dot_general(bg, xw2[:, GW * g:GW * (g + 1)], (((0,), (0,)), ((), ())),
                               preferred_element_type=F32)
        state_ref[g] = st_prev * cd_e[:, GW * g:GW * (g + 1)] + st_c
        yd = []
        for pj in range(GW // LANES):
            h0 = (GW // SSM_HEAD_DIM) * g + 2 * pj
            ms = []
            for h in (h0, h0 + 1):
                diff = acum[:, h:h + 1] - acumt[h:h + 1, :]
                ms.append((cb * jnp.exp(jnp.where(tri, diff, NEG_BIG))).astype(BF16))
            xp = xdt[:, LANES * (h0 // 2):LANES * (h0 // 2 + 1)]
            zero = jnp.zeros_like(xp)
            rhs = jnp.concatenate([jnp.where(low, xp, zero), jnp.where(low, zero, xp)], axis=0)
            yd.append(jnp.dot(jnp.concatenate(ms, axis=1), rhs, preferred_element_type=F32))
        ys.append(jnp.concatenate(yd, axis=1) + y_off)
    y = jnp.concatenate(ys, axis=1) + dexp_ref[...] * xs

    zz = z_ref[...]
    uu = y * (zz * _sigmoid(zz))
    outs = []
    for g in range(SSM_GROUPS):
        ug = uu[:, GW * g:GW * (g + 1)]
        ms_ = jnp.mean(ug * ug, axis=-1, keepdims=True)
        outs.append(ug * lax.rsqrt(ms_ + RMS_EPS))
    o_ref[...] = (jnp.concatenate(outs, axis=1) * nw_ref[...]).astype(o_ref.dtype)


def _ssd(z, xbc, sm, smt, conv_w, conv_b, dt_bias, a_log, d_skip, norm_w, bsz, seq):
    L = SSM_CHUNK
    nc = seq // L
    t = bsz * seq
    dtbrow = jnp.zeros((1, SMALL_W), F32).at[0, DT_OFF:DT_OFF + SSM_HEADS].set(dt_bias.astype(F32))
    arow = jnp.zeros((1, SMALL_W), F32).at[0, DT_OFF:DT_OFF + SSM_HEADS].set(a_log.astype(F32))
    dtbcol = jnp.zeros((SMALL_T, 1), F32).at[DT_OFF:DT_OFF + SSM_HEADS, 0].set(dt_bias.astype(F32))
    acol = jnp.zeros((SMALL_T, 1), F32).at[DT_OFF:DT_OFF + SSM_HEADS, 0].set(a_log.astype(F32))
    dtbcol = jnp.broadcast_to(dtbcol, (SMALL_T, L))
    acol = jnp.broadcast_to(acol, (SMALL_T, L))
    dexp = jnp.repeat(d_skip.astype(F32), SSM_HEAD_DIM)[None, :]
    e1 = np.zeros((SMALL_W, SSM_INNER), np.float32)
    for h in range(SSM_HEADS):
        e1[DT_OFF + h, SSM_HEAD_DIM * h:SSM_HEAD_DIM * (h + 1)] = 1.0
    e2 = jnp.asarray(np.concatenate([e1, e1], axis=0), BF16)
    win = CONV_PAD + L
    sh = np.zeros((L, (SSM_CONV - 1) * win), np.float32)
    for s in range(1, SSM_CONV):
        sh[np.arange(L), (s - 1) * win + CONV_PAD + np.arange(L) - s] = 1.0
    shift = jnp.asarray(sh, BF16)
    tok = lambda w: pl.BlockSpec((L, w), lambda b, c: (b * nc + c, 0))
    return pl.pallas_call(
        _ssd_kernel,
        grid=(bsz, nc),
        in_specs=[
            tok(SSM_INNER), tok(SSM_CONV_DIM), tok(SMALL_W),
            pl.BlockSpec((SMALL_T, L), lambda b, c: (0, b * nc + c)),
            _const_spec((SSM_CONV, SSM_CONV_DIM)), _const_spec((SSM_CONV, SSM_CONV_DIM)),
            _const_spec((1, SSM_CONV_DIM)), _const_spec(shift.shape),
            _const_spec(dtbrow.shape), _const_spec(dtbcol.shape), _const_spec(arow.shape), _const_spec(acol.shape),
            _const_spec(dexp.shape), _const_spec((1, SSM_INNER)), _const_spec(e2.shape),
        ],
        out_specs=tok(SSM_INNER),
        out_shape=jax.ShapeDtypeStruct((t, SSM_INNER), BF16),
        scratch_shapes=[pltpu.VMEM((CONV_PAD + L, SSM_CONV_DIM), BF16),
                        pltpu.VMEM((SSM_GROUPS, SSM_STATE, GROUP_WIDTH), F32)],
        compiler_params=_params("arbitrary", "arbitrary"),
        name="ssd",
    )(z, xbc, sm, smt, conv_w.astype(F32), conv_w.astype(BF16), conv_b.astype(F32)[None, :], shift,
      dtbrow, dtbcol, arow, acol, dexp, norm_w.astype(F32)[None, :], e2)


def _mix_kernel(attn_ref, ssm_ref, gl_ref, x_ref, wpa_ref, wps_ref, wout_ref, bg_ref, g_ref, b_ref, o_ref):
    rows = _row_groups(x_ref.shape[0])
    ad = [jnp.dot(attn_ref[r, :], wpa_ref[...], preferred_element_type=F32) for r in rows]
    sd = [jnp.dot(ssm_ref[r, :], wps_ref[...], preferred_element_type=F32) for r in rows]
    for k, r in enumerate(rows):
        gates = _sigmoid(gl_ref[r, :] + bg_ref[...])
        mix = gates[:, :D_MODEL] * ad[k] + gates[:, D_MODEL:] * sd[k]
        mixed = jnp.dot(mix.astype(BF16), wout_ref[...], preferred_element_type=F32)
        o_ref[r, :] = _layer_norm(DEEPNORM_ALPHA * x_ref[r, :] + mixed, g_ref[...], b_ref[...])


def _mix(attn, ssm, gl, x2, wpa, wps, wout, b_gates, ln_g, ln_b):
    t = x2.shape[0]
    tm = TM_MIX
    tok = lambda w: pl.BlockSpec((tm, w), lambda i: (i, 0))
    return pl.pallas_call(
        _mix_kernel,
        grid=(t // tm,),
        in_specs=[tok(ATT_WIDTH), tok(SSM_INNER), tok(2 * D_MODEL), tok(D_MODEL),
                  _const_spec(wpa.shape), _const_spec(wps.shape), _const_spec(wout.shape),
                  _const_spec((1, 2 * D_MODEL)), _const_spec((1, D_MODEL)), _const_spec((1, D_MODEL))],
        out_specs=tok(D_MODEL),
        out_shape=jax.ShapeDtypeStruct((t, D_MODEL), F32),
        compiler_params=_params("arbitrary"),
        name="mix",
    )(attn, ssm, gl, x2, wpa, wps, wout, b_gates.astype(F32)[None, :], ln_g.astype(F32)[None, :],
      ln_b.astype(F32)[None, :])


def _ffn_kernel(x_ref, wg_ref, wu_ref, wd_ref, g_ref, b_ref, o_ref):
    rows = _row_groups(x_ref.shape[0])
    xb = [x_ref[r, :].astype(BF16) for r in rows]
    hg = [jnp.dot(v, wg_ref[...], preferred_element_type=F32) for v in xb]
    hu = [jnp.dot(v, wu_ref[...], preferred_element_type=F32) for v in xb]
    for k, r in enumerate(rows):
        h = (hg[k] * _sigmoid(hg[k]) * hu[k]).astype(BF16)
        d = jnp.dot(h, wd_ref[...], preferred_element_type=F32)
        o_ref[r, :] = _layer_norm(DEEPNORM_ALPHA * x_ref[r, :] + d, g_ref[...], b_ref[...])


def _ffn(x1, wg, wu, wd, ln_g, ln_b):
    t = x1.shape[0]
    tm = TM_FFN
    tok = pl.BlockSpec((tm, D_MODEL), lambda i: (i, 0))
    return pl.pallas_call(
        _ffn_kernel,
        grid=(t // tm,),
        in_specs=[tok, _const_spec(wg.shape), _const_spec(wu.shape), _const_spec(wd.shape),
                  _const_spec((1, D_MODEL)), _const_spec((1, D_MODEL))],
        out_specs=tok,
        out_shape=jax.ShapeDtypeStruct((t, D_MODEL), F32),
        compiler_params=_params("arbitrary"),
        name="ffn",
    )(x1, wg, wu, wd, ln_g.astype(F32)[None, :], ln_b.astype(F32)[None, :])


def _pack_in_weights(w):
    o = IN_OFFS
    q, k, v, f = w[:, o[0]:o[1]], w[:, o[1]:o[2]], w[:, o[2]:o[3]], w[:, o[3]:o[4]]
    z, xbc, dt, gate = w[:, o[4]:o[5]], w[:, o[5]:o[6]], w[:, o[6]:o[7]], w[:, o[7]:o[8]]
    assert DT_OFF == 0 and F_OFF == SSM_HEADS
    bf = lambda a: a.astype(BF16)
    small = jnp.concatenate([bf(dt), bf(f), jnp.zeros((D_MODEL, SMALL_W - SSM_HEADS - ATT_HEADS), BF16)], axis=1)
    scale = LOG2E / math.sqrt(ATT_HEAD_DIM)
    return bf(q * scale), bf(k), bf(v), bf(z), bf(xbc), bf(gate), small


def kernel(x, w_in, b_forget, conv_w, conv_b, dt_bias, a_log, d_skip, ssm_norm_w, w_proj_attn,
           w_proj_ssm, b_gates, w_out, ln1_g, ln1_b, w_ffn_gate, w_ffn_up, w_ffn_down, ln2_g, ln2_b):
    bsz, seq, dm = x.shape
    assert dm == D_MODEL and seq % ATT_BLOCK == 0 and seq % SSM_CHUNK == 0
    assert w_in.shape[0] == DEPTH
    x2 = x.reshape(bsz * seq, dm)
    for l in range(DEPTH):
        k, z, xbc, gl, sm, qt, vt, smt = _inproj(x2, _pack_in_weights(w_in[l]))
        ka, xq, base = _attnprep(sm, smt, k, b_forget[l], bsz, seq)
        attn = _attention(base, qt, xq, ka, vt, bsz, seq)
        ssm = _ssd(z, xbc, sm, smt, conv_w[l], conv_b[l], dt_bias[l], a_log[l], d_skip[l], ssm_norm_w[l],
                   bsz, seq)
        x1 = _mix(attn, ssm, gl, x2, w_proj_attn[l].astype(BF16), w_proj_ssm[l].astype(BF16),
                  w_out[l].astype(BF16), b_gates[l], ln1_g[l], ln1_b[l])
        x2 = _ffn(x1, w_ffn_gate[l].astype(BF16), w_ffn_up[l].astype(BF16), w_ffn_down[l].astype(BF16),
                  ln2_g[l], ln2_b[l])
    return x2.reshape(bsz, seq, dm)
```

```python
import functools
import math

import numpy as np
import jax
import jax.numpy as jnp
from jax import lax
from jax.experimental import pallas as pl
from jax.experimental.pallas import tpu as pltpu

F32 = jnp.float32
BF16 = jnp.bfloat16

D_MODEL = 1024
ATT_HEADS = 16
ATT_HEAD_DIM = 64
ATT_WIDTH = ATT_HEADS * ATT_HEAD_DIM
SSM_INNER = 2048
SSM_HEAD_DIM = 64
SSM_HEADS = 32
SSM_GROUPS = 4
SSM_STATE = 128
SSM_CONV = 4
SSM_CHUNK = 128
SSM_CONV_DIM = SSM_INNER + 2 * SSM_GROUPS * SSM_STATE
GROUP_WIDTH = SSM_INNER // SSM_GROUPS
FFN_HIDDEN = 2816
DEPTH = 1
DEEPNORM_ALPHA = (2 * DEPTH) ** 0.25
LN_EPS = 1e-5
RMS_EPS = 1e-5
IN_SIZES = (ATT_WIDTH, ATT_WIDTH, ATT_WIDTH, ATT_HEADS, SSM_INNER, SSM_CONV_DIM, SSM_HEADS, 2 * D_MODEL)
IN_OFFS = tuple(int(v) for v in np.concatenate([[0], np.cumsum(IN_SIZES)]))

LANES = 128
VMEM_LIMIT = 56 * 1024 * 1024

TM_INPROJ = 256
ATT_BLOCK = 512
ATT_QBLOCK = 1024
TM_MIX = 512
TM_FFN = 512
ROW_GROUPS = 2
SMALL_W = 128
SMALL_T = 64
DT_OFF = 0
F_OFF = 32
XQ_ROWS = 16
ACC_ROWS = ATT_HEAD_DIM + 16
ATT_ROW_CHUNK = 64
ATT_TRIP = 4
SSD_SUB = 4
CONV_PAD = 16
LOG2E = math.log2(math.e)
NEG_BIG = -1e30


def _sigmoid(v):
    return 1.0 / (1.0 + jnp.exp(-v))


def _softplus(v):
    return jnp.maximum(v, 0.0) + jnp.log1p(jnp.exp(-jnp.abs(v)))


def _split2(v):
    hi = v.astype(BF16)
    lo = (v - hi.astype(F32)).astype(BF16)
    return hi, lo


def _split3(v):
    hi = v.astype(BF16)
    r = v - hi.astype(F32)
    lo = r.astype(BF16)
    lolo = (r - lo.astype(F32)).astype(BF16)
    return hi, lo, lolo


def _scan_rows(v):
    n = v.shape[0]
    row = lax.broadcasted_iota(jnp.int32, v.shape, 0)
    d = 1
    while d < n:
        v = v + jnp.where(row >= d, pltpu.roll(v, d, 0), 0.0)
        d *= 2
    return v


def _scan_lanes(v):
    n = v.shape[1]
    col = lax.broadcasted_iota(jnp.int32, v.shape, 1)
    d = 1
    while d < n:
        v = v + jnp.where(col >= d, pltpu.roll(v, d, 1), 0.0)
        d *= 2
    return v


def _layer_norm(y, g, b):
    mu = jnp.mean(y, axis=-1, keepdims=True)
    yc = y - mu
    var = jnp.mean(yc * yc, axis=-1, keepdims=True)
    return yc * lax.rsqrt(var + LN_EPS) * g + b


def _row_groups(n):
    g = n // ROW_GROUPS
    return [slice(k * g, (k + 1) * g) for k in range(ROW_GROUPS)]


def _const_spec(shape):
    nd = len(shape)
    return pl.BlockSpec(shape, lambda *_: (0,) * nd, pipeline_mode=pl.Buffered(1))


def _params(*sem):
    return pltpu.CompilerParams(dimension_semantics=sem, vmem_limit_bytes=VMEM_LIMIT)


def _inproj_kernel(x_ref, wq_ref, wk_ref, wv_ref, wz_ref, wxbc_ref, wg_ref, ws_ref,
                   k_ref, z_ref, xbc_ref, gl_ref, sm_ref, qt_ref, vt_ref, smt_ref):
    xb = x_ref[...].astype(BF16)

    def proj(w_ref):
        return jnp.dot(xb, w_ref[...], preferred_element_type=F32)

    k_ref[...] = proj(wk_ref).astype(BF16)
    z_ref[...] = proj(wz_ref)
    xbc_ref[...] = proj(wxbc_ref).astype(BF16)
    gl_ref[...] = proj(wg_ref)
    sm = proj(ws_ref)
    sm_ref[...] = sm
    qt_ref[...] = proj(wq_ref).T.astype(BF16)
    vt_ref[...] = proj(wv_ref).T.astype(BF16)
    smt_ref[...] = sm.T[0:SMALL_T, :]


def _inproj(x2, weights):
    t = x2.shape[0]
    tm = TM_INPROJ
    tok = lambda w: pl.BlockSpec((tm, w), lambda i: (i, 0))
    feat = lambda h: pl.BlockSpec((h, tm), lambda i: (0, i))
    return pl.pallas_call(
        _inproj_kernel,
        grid=(t // tm,),
        in_specs=[tok(D_MODEL)] + [_const_spec(w.shape) for w in weights],
        out_specs=[tok(1024), tok(2048), tok(3072), tok(2048), tok(SMALL_W),
                   feat(1024), feat(1024), feat(SMALL_T)],
        out_shape=[
            jax.ShapeDtypeStruct((t, 1024), BF16),
            jax.ShapeDtypeStruct((t, 2048), F32),
            jax.ShapeDtypeStruct((t, 3072), BF16),
            jax.ShapeDtypeStruct((t, 2048), F32),
            jax.ShapeDtypeStruct((t, SMALL_W), F32),
            jax.ShapeDtypeStruct((1024, t), BF16),
            jax.ShapeDtypeStruct((1024, t), BF16),
            jax.ShapeDtypeStruct((SMALL_T, t), F32),
        ],
        compiler_params=_params("arbitrary"),
        name="inproj",
    )(x2, *weights)


def _attnprep_kernel(sm_ref, smt_ref, k_ref, bfrow_ref, bfcol_ref, shi_ref, crow_ref,
                     phi_ref, plo_ref, pll_ref, ka_ref, xq_ref, base_ref, carry_ref):
    i = pl.program_id(1)

    @pl.when(i == 0)
    def _():
        carry_ref[...] = jnp.zeros_like(carry_ref)

    g = sm_ref.shape[0]
    v = sm_ref[...] + bfrow_ref[...]
    lf = -_softplus(-v) * LOG2E
    rk = _scan_rows(lf)
    base_ref[0] = carry_ref[...]
    carry_ref[...] = carry_ref[...] + rk[g - 1:g, :]
    lane = lax.broadcasted_iota(jnp.int32, (g, LANES), 1)
    low = lane < ATT_HEAD_DIM
    heads = (lane >= F_OFF) & (lane < F_OFF + ATT_HEADS)
    hi, lo, ll = (jnp.where(heads, part.astype(F32), 0.0) for part in _split3(rk))
    packed = hi + pltpu.roll(lo, ATT_HEADS, 1) + pltpu.roll(ll, 2 * ATT_HEADS, 1)
    extras = jnp.dot(packed.astype(BF16), shi_ref[...], preferred_element_type=F32) + crow_ref[...]
    for j in range(ATT_HEADS // 2):
        kk = k_ref[:, LANES * j:LANES * (j + 1)].astype(F32)
        ev = extras[:, 2 * LANES * j:2 * LANES * j + LANES]
        od = extras[:, 2 * LANES * j + LANES:2 * LANES * (j + 1)]
        ka_ref[:, 2 * LANES * j:2 * LANES * j + LANES] = jnp.where(low, kk, ev).astype(BF16)
        ka_ref[:, 2 * LANES * j + LANES:2 * LANES * (j + 1)] = jnp.where(low, od, kk).astype(BF16)

    vt = smt_ref[...] + bfcol_ref[...]
    lft = -_softplus(-vt) * LOG2E
    rq = _scan_lanes(lft)
    hq, lq, llq = _split3(rq)
    nrow = ATT_HEADS * XQ_ROWS
    row = lax.broadcasted_iota(jnp.int32, (nrow, g), 0)
    ones = jnp.where((row & (XQ_ROWS - 1)) < 3, 1.0, 0.0)
    xq = (jnp.dot(phi_ref[...], hq, preferred_element_type=F32)
          + jnp.dot(plo_ref[...], lq, preferred_element_type=F32)
          + jnp.dot(pll_ref[...], llq, preferred_element_type=F32)
          + ones)
    xq_ref[...] = xq.astype(BF16)


def _attnprep_consts():
    assert F_OFF + 3 * ATT_HEADS <= SMALL_W
    s_hi = np.zeros((SMALL_W, ATT_HEADS * LANES), np.float32)
    crow = np.zeros((1, ATT_HEADS * LANES), np.float32)
    p_hi = np.zeros((ATT_HEADS * XQ_ROWS, SMALL_T), np.float32)
    p_lo = np.zeros_like(p_hi)
    p_ll = np.zeros_like(p_hi)
    for h in range(ATT_HEADS):
        off = LANES * h + (ATT_HEAD_DIM if h % 2 == 0 else 0)
        for part in range(3):
            s_hi[F_OFF + part * ATT_HEADS + h, off + part] = -1.0
        crow[0, off + 3:off + 6] = 1.0
        p_hi[XQ_ROWS * h + 3, F_OFF + h] = 1.0
        p_lo[XQ_ROWS * h + 4, F_OFF + h] = 1.0
        p_ll[XQ_ROWS * h + 5, F_OFF + h] = 1.0
    b = lambda a: jnp.asarray(a, BF16)
    return b(s_hi), jnp.asarray(crow), b(p_hi), b(p_lo), b(p_ll)


def _attnprep(sm, smt, k, b_forget, bsz, seq):
    g = ATT_BLOCK
    nb = seq // g
    s_hi, crow, p_hi, p_lo, p_ll = _attnprep_consts()
    bfrow = jnp.zeros((1, SMALL_W), F32).at[0, F_OFF:F_OFF + ATT_HEADS].set(b_forget.astype(F32))
    bfcol = jnp.zeros((SMALL_T, 1), F32).at[F_OFF:F_OFF + ATT_HEADS, 0].set(b_forget.astype(F32))
    bfcol = jnp.broadcast_to(bfcol, (SMALL_T, g))
    t = bsz * seq
    ka, xq, base = pl.pallas_call(
        _attnprep_kernel,
        grid=(bsz, nb),
        in_specs=[
            pl.BlockSpec((g, SMALL_W), lambda b, i: (b * nb + i, 0)),
            pl.BlockSpec((SMALL_T, g), lambda b, i: (0, b * nb + i)),
            pl.BlockSpec((g, ATT_WIDTH), lambda b, i: (b * nb + i, 0)),
            _const_spec(bfrow.shape), _const_spec(bfcol.shape),
            _const_spec(s_hi.shape), _const_spec(crow.shape),
            _const_spec(p_hi.shape), _const_spec(p_lo.shape), _const_spec(p_ll.shape),
        ],
        out_specs=[
            pl.BlockSpec((g, ATT_HEADS * LANES), lambda b, i: (b * nb + i, 0)),
            pl.BlockSpec((ATT_HEADS * XQ_ROWS, g), lambda b, i: (0, b * nb + i)),
            pl.BlockSpec((1, 1, SMALL_W), lambda b, i: (b * nb + i, 0, 0)),
        ],
        out_shape=[
            jax.ShapeDtypeStruct((t, ATT_HEADS * LANES), BF16),
            jax.ShapeDtypeStruct((ATT_HEADS * XQ_ROWS, t), BF16),
            jax.ShapeDtypeStruct((bsz * nb, 1, SMALL_W), F32),
        ],
        scratch_shapes=[pltpu.VMEM((1, SMALL_W), F32)],
        compiler_params=_params("arbitrary", "arbitrary"),
        name="attnprep",
    )(sm, smt, k, bfrow, bfcol, s_hi, crow, p_hi, p_lo, p_ll)
    base = base.reshape(bsz, nb, SMALL_W)[:, :, F_OFF:F_OFF + ATT_HEADS]
    base = jnp.transpose(base, (0, 2, 1)).reshape(-1)
    return ka, xq, base


def _attn_kernel(base_ref, qt_ref, xq_ref, ka_ref, vt_ref, o_ref, *scratch, nb):
    b = pl.program_id(0)
    pr = pl.program_id(1)
    i = pl.program_id(2)
    tq = qt_ref.shape[1]
    tk = ATT_BLOCK
    dh = ATT_HEAD_DIM
    s_slots = tuple(scratch[2 * d:2 * d + 2] for d in range(ATT_TRIP))
    p_refs = scratch[2 * ATT_TRIP:2 * ATT_TRIP + 2]
    acc_refs = scratch[2 * ATT_TRIP + 2:2 * ATT_TRIP + 4]
    m_refs = scratch[2 * ATT_TRIP + 4:2 * ATT_TRIP + 6]
    zpad = jnp.zeros((LANES - dh - XQ_ROWS, tq), BF16)
    qa = (jnp.concatenate([qt_ref[0:dh, :], xq_ref[0:XQ_ROWS, :], zpad], axis=0),
          jnp.concatenate([xq_ref[XQ_ROWS:2 * XQ_ROWS, :], zpad, qt_ref[dh:2 * dh, :]], axis=0))
    boff = [(b * ATT_HEADS + 2 * pr + e) * nb for e in range(2)]
    ones_rows = jnp.ones((ACC_ROWS - dh, tk), BF16)
    for e in range(2):
        acc_refs[e][...] = jnp.zeros_like(acc_refs[e])
        m_refs[e][...] = jnp.full(m_refs[e].shape, NEG_BIG, F32)
    qlane = lax.broadcasted_iota(jnp.int32, (1, tq), 1)
    qbase = [jnp.where(qlane >= tk, base_ref[boff[e] + 2 * i + 1], base_ref[boff[e] + 2 * i]) for e in range(2)]

    def scores(j, s_ref, e, lo):
        off = pl.multiple_of(j * tk, tk)
        kj = ka_ref[pl.ds(off, tk), LANES * e:LANES * (e + 1)]
        s = jnp.dot(kj, qa[e][:, lo:], preferred_element_type=F32)
        s_ref[:, lo:] = s
        return jnp.max(s, axis=0, keepdims=True)

    def accumulate(j, s_ref, mt, m, e, lo, masked):
        off = pl.multiple_of(j * tk, tk)
        w = tq - lo
        c = qbase[e][:, lo:] - base_ref[boff[e] + j]
        p_ref, acc_ref = p_refs[e], acc_refs[e]
        if masked:
            mt = None
            for r in range(tk // ATT_ROW_CHUNK):
                rows = slice(r * ATT_ROW_CHUNK, (r + 1) * ATT_ROW_CHUNK)
                krow = lax.broadcasted_iota(jnp.int32, (ATT_ROW_CHUNK, w), 0) + r * ATT_ROW_CHUNK
                qcol = lax.broadcasted_iota(jnp.int32, (ATT_ROW_CHUNK, w), 1)
                sc = jnp.where(krow <= qcol, s_ref[rows, lo:], NEG_BIG)
                s_ref[rows, lo:] = sc
                cm = jnp.max(sc, axis=0, keepdims=True)
                mt = cm if mt is None else jnp.maximum(mt, cm)
        m_old = m[:, lo:]
        m_new = jnp.maximum(m_old, mt + c)
        alpha = jnp.exp2(m_old - m_new)
        sh = m_new - c
        for r in range(tk // ATT_ROW_CHUNK):
            rows = slice(r * ATT_ROW_CHUNK, (r + 1) * ATT_ROW_CHUNK)
            p_ref[rows, lo:] = jnp.exp2(s_ref[rows, lo:] - sh).astype(BF16)
        vj = vt_ref[dh * e:dh * (e + 1), pl.ds(off, tk)]
        vaug = jnp.concatenate([vj, ones_rows], axis=0)
        acc_ref[:, lo:] = alpha * acc_ref[:, lo:] + jnp.dot(vaug, p_ref[:, lo:], preferred_element_type=F32)
        return m_new if lo == 0 else jnp.concatenate([m[:, :lo], m_new], axis=1)

    def trip(blocks):
        los = [0 if d is None else d * tk for _, d in blocks]
        mts = [[scores(j, s_slots[k][e], e, los[k]) for e in range(2)] for k, (j, _) in enumerate(blocks)]
        ms = [m_refs[e][0:1, :] for e in range(2)]
        for k, (j, d) in enumerate(blocks):
            ms = [accumulate(j, s_slots[k][e], mts[k][e], ms[e], e, los[k], d is not None) for e in range(2)]
        for e in range(2):
            m_refs[e][0:1, :] = ms[e]

    def full_trip(t, carry):
        trip([(ATT_TRIP * t + k, None) for k in range(ATT_TRIP)])
        return carry

    assert ATT_TRIP == 4
    lax.fori_loop(0, (2 * i) // ATT_TRIP, full_trip, 0)
    diag_blocks = [(2 * i, 0), (2 * i + 1, 1)]

    @pl.when(i % 2 == 0)
    def _():
        trip(diag_blocks)

    @pl.when(i % 2 == 1)
    def _():
        trip([(2 * i - 2, None), (2 * i - 1, None)] + diag_blocks)

    ot = jnp.concatenate([a_ref[0:dh, :] / a_ref[dh:dh + 1, :] for a_ref in acc_refs], axis=0)
    o_ref[...] = ot.T.astype(o_ref.dtype)


def _attention(base, qt, xq, ka, vt, bsz, seq):
    tq = ATT_QBLOCK
    tk = ATT_BLOCK
    assert tq == 2 * tk and seq % tq == 0
    nq = seq // tq
    nb = seq // tk
    npair = ATT_HEADS // 2
    t = bsz * seq
    return pl.pallas_call(
        functools.partial(_attn_kernel, nb=nb),
        grid=(bsz, npair, nq),
        in_specs=[
            pl.BlockSpec(memory_space=pltpu.SMEM),
            pl.BlockSpec((2 * ATT_HEAD_DIM, tq), lambda b, p, i: (p, b * nq + i)),
            pl.BlockSpec((2 * XQ_ROWS, tq), lambda b, p, i: (p, b * nq + i)),
            pl.BlockSpec((seq, 2 * LANES), lambda b, p, i: (b, p)),
            pl.BlockSpec((2 * ATT_HEAD_DIM, seq), lambda b, p, i: (p, b)),
        ],
        out_specs=pl.BlockSpec((tq, 2 * ATT_HEAD_DIM), lambda b, p, i: (b * nq + i, p)),
        out_shape=jax.ShapeDtypeStruct((t, ATT_WIDTH), BF16),
        scratch_shapes=([pltpu.VMEM((tk, tq), F32)] * (2 * ATT_TRIP)
                        + [pltpu.VMEM((tk, tq), BF16)] * 2
                        + [pltpu.VMEM((ACC_ROWS, tq), F32)] * 2
                        + [pltpu.VMEM((8, tq), F32)] * 2),
        compiler_params=_params("arbitrary", "arbitrary", "arbitrary"),
        name="attn",
    )(base, qt, xq, ka, vt)


def _ssd_kernel(z_ref, xbc_ref, sm_ref, smt_ref, cw_ref, cwb_ref, cb_ref, shift_ref, dtbrow_ref, dtbcol_ref,
                arow_ref, acol_ref, dexp_ref, nw_ref, e2_ref, o_ref, prev_ref, state_ref):
    c = pl.program_id(1)
    L = SSM_CHUNK

    @pl.when(c == 0)
    def _():
        prev_ref[0:CONV_PAD, :] = jnp.zeros((CONV_PAD, SSM_CONV_DIM), BF16)
        state_ref[...] = jnp.zeros_like(state_ref)

    prev_ref[CONV_PAD:, :] = xbc_ref[...]
    for sub in range(SSD_SUB):
        _ssd_chunk(slice(sub * L, (sub + 1) * L), z_ref, xbc_ref, sm_ref, smt_ref, cw_ref, cwb_ref, cb_ref,
                   shift_ref, dtbrow_ref, dtbcol_ref, arow_ref, acol_ref, dexp_ref, nw_ref, e2_ref, o_ref,
                   prev_ref, state_ref)
    prev_ref[0:CONV_PAD, :] = xbc_ref[SSD_SUB * L - CONV_PAD:SSD_SUB * L, :]


def _ssd_chunk(rows, z_ref, xbc_ref, sm_ref, smt_ref, cw_ref, cwb_ref, cb_ref, shift_ref, dtbrow_ref, dtbcol_ref,
               arow_ref, acol_ref, dexp_ref, nw_ref, e2_ref, o_ref, prev_ref, state_ref):
    L = SSM_CHUNK
    N = SSM_STATE
    GW = GROUP_WIDTH
    ub = xbc_ref[rows, :]
    win = prev_ref[rows.start:rows.start + CONV_PAD + L, :]
    taps = jnp.concatenate([win * cwb_ref[SSM_CONV - 1 - s:SSM_CONV - s, :] for s in range(1, SSM_CONV)],
                           axis=0)
    acc = (ub.astype(F32) * cw_ref[SSM_CONV - 1:SSM_CONV, :] + cb_ref[...]
           + jnp.dot(shift_ref[...], taps, preferred_element_type=F32))
    xc = acc * _sigmoid(acc)
    xs = xc[:, :SSM_INNER]
    bm = xc[:, SSM_INNER:SSM_INNER + SSM_GROUPS * N].astype(BF16)
    cm = xc[:, SSM_INNER + SSM_GROUPS * N:].astype(BF16)

    a_row = -jnp.exp(arow_ref[...])
    dt = _softplus(sm_ref[rows, :] + dtbrow_ref[...])
    acum = _scan_rows(dt * a_row)
    a_col = -jnp.exp(acol_ref[...])
    dtt = _softplus(smt_ref[:, rows] + dtbcol_ref[...])
    acumt = _scan_lanes(dtt * a_col)
    acum2, acumt2 = acum * LOG2E, acumt * LOG2E
    alast = acum[L - 1:L, :]
    ea = jnp.exp(acum)
    w2 = dt * jnp.exp(alast - acum)

    def expand(v):
        hi, lo = _split2(v)
        return jnp.dot(jnp.concatenate([hi, lo], axis=1), e2_ref[...], preferred_element_type=F32)

    dt_e = expand(dt)
    ea_e = expand(ea)
    w2_e = expand(w2)
    cd_e = ea_e[L - 1:L, :]

    xdt = (xs * dt_e).astype(BF16)
    xw2 = (xs * w2_e).astype(BF16)

    r2 = lax.broadcasted_iota(jnp.int32, (L, L), 0)
    c2 = lax.broadcasted_iota(jnp.int32, (L, L), 1)
    tri = r2 >= c2
    lane = lax.broadcasted_iota(jnp.int32, (L, LANES), 1)
    low = lane < SSM_HEAD_DIM

    ys = []
    for g in range(SSM_GROUPS):
        bg = bm[:, N * g:N * (g + 1)]
        cg = cm[:, N * g:N * (g + 1)]
        cb = lax.dot_general(cg, bg, (((1,), (1,)), ((), ())), preferred_element_type=F32)
        st_prev = state_ref[g]
        y_off = jnp.dot(cg, st_prev.astype(BF16), preferred_element_type=F32) * ea_e[:, GW * g:GW * (g + 1)]
        st_c = lax.dot_general(bg, xw2[:, GW * g:GW * (g + 1)], (((0,), (0,)), ((), ())),
                               preferred_element_type=F32)
        state_ref[g] = st_prev * cd_e[:, GW * g:GW * (g + 1)] + st_c
        yd = []
        for pj in range(GW // LANES):
            h0 = (GW // SSM_HEAD_DIM) * g + 2 * pj
            ms = []
            for h in (h0, h0 + 1):
                diff = acum2[:, h:h + 1] - acumt2[h:h + 1, :]
                ms.append((cb * jnp.exp2(jnp.where(tri, diff, NEG_BIG))).astype(BF16))
            xp = xdt[:, LANES * (h0 // 2):LANES * (h0 // 2 + 1)]
            zero = jnp.zeros_like(xp)
            rhs = jnp.concatenate([jnp.where(low, xp, zero), jnp.where(low, zero, xp)], axis=0)
            yd.append(jnp.dot(jnp.concatenate(ms, axis=1), rhs, preferred_element_type=F32))
        ys.append(jnp.concatenate(yd, axis=1) + y_off)
    y = jnp.concatenate(ys, axis=1) + dexp_ref[...] * xs

    zz = z_ref[rows, :]
    uu = y * (zz * _sigmoid(zz))
    outs = []
    for g in range(SSM_GROUPS):
        ug = uu[:, GW * g:GW * (g + 1)]
        ms_ = jnp.mean(ug * ug, axis=-1, keepdims=True)
        outs.append(ug * lax.rsqrt(ms_ + RMS_EPS))
    o_ref[rows, :] = (jnp.concatenate(outs, axis=1) * nw_ref[...]).astype(o_ref.dtype)


def _ssd(z, xbc, sm, smt, conv_w, conv_b, dt_bias, a_log, d_skip, norm_w, bsz, seq):
    L = SSM_CHUNK
    rows = SSD_SUB * L
    assert seq % rows == 0
    nc = seq // rows
    t = bsz * seq
    dtbrow = jnp.zeros((1, SMALL_W), F32).at[0, DT_OFF:DT_OFF + SSM_HEADS].set(dt_bias.astype(F32))
    arow = jnp.zeros((1, SMALL_W), F32).at[0, DT_OFF:DT_OFF + SSM_HEADS].set(a_log.astype(F32))
    dtbcol = jnp.zeros((SMALL_T, 1), F32).at[DT_OFF:DT_OFF + SSM_HEADS, 0].set(dt_bias.astype(F32))
    acol = jnp.zeros((SMALL_T, 1), F32).at[DT_OFF:DT_OFF + SSM_HEADS, 0].set(a_log.astype(F32))
    dtbcol = jnp.broadcast_to(dtbcol, (SMALL_T, L))
    acol = jnp.broadcast_to(acol, (SMALL_T, L))
    dexp = jnp.repeat(d_skip.astype(F32), SSM_HEAD_DIM)[None, :]
    e1 = np.zeros((SMALL_W, SSM_INNER), np.float32)
    for h in range(SSM_HEADS):
        e1[DT_OFF + h, SSM_HEAD_DIM * h:SSM_HEAD_DIM * (h + 1)] = 1.0
    e2 = jnp.asarray(np.concatenate([e1, e1], axis=0), BF16)
    win = CONV_PAD + L
    sh = np.zeros((L, (SSM_CONV - 1) * win), np.float32)
    for s in range(1, SSM_CONV):
        sh[np.arange(L), (s - 1) * win + CONV_PAD + np.arange(L) - s] = 1.0
    shift = jnp.asarray(sh, BF16)
    tok = lambda w: pl.BlockSpec((rows, w), lambda b, c: (b * nc + c, 0))
    return pl.pallas_call(
        _ssd_kernel,
        grid=(bsz, nc),
        in_specs=[
            tok(SSM_INNER), tok(SSM_CONV_DIM), tok(SMALL_W),
            pl.BlockSpec((SMALL_T, rows), lambda b, c: (0, b * nc + c)),
            _const_spec((SSM_CONV, SSM_CONV_DIM)), _const_spec((SSM_CONV, SSM_CONV_DIM)),
            _const_spec((1, SSM_CONV_DIM)), _const_spec(shift.shape),
            _const_spec(dtbrow.shape), _const_spec(dtbcol.shape), _const_spec(arow.shape), _const_spec(acol.shape),
            _const_spec(dexp.shape), _const_spec((1, SSM_INNER)), _const_spec(e2.shape),
        ],
        out_specs=tok(SSM_INNER),
        out_shape=jax.ShapeDtypeStruct((t, SSM_INNER), BF16),
        scratch_shapes=[pltpu.VMEM((CONV_PAD + rows, SSM_CONV_DIM), BF16),
                        pltpu.VMEM((SSM_GROUPS, SSM_STATE, GROUP_WIDTH), F32)],
        compiler_params=_params("arbitrary", "arbitrary"),
        name="ssd",
    )(z, xbc, sm, smt, conv_w.astype(F32), conv_w.astype(BF16), conv_b.astype(F32)[None, :], shift,
      dtbrow, dtbcol, arow, acol, dexp, norm_w.astype(F32)[None, :], e2)


def _mix_kernel(attn_ref, ssm_ref, gl_ref, x_ref, wpa_ref, wps_ref, wout_ref, bg_ref, g_ref, b_ref, o_ref):
    rows = _row_groups(x_ref.shape[0])
    ad = [jnp.dot(attn_ref[r, :], wpa_ref[...], preferred_element_type=F32) for r in rows]
    sd = [jnp.dot(ssm_ref[r, :], wps_ref[...], preferred_element_type=F32) for r in rows]
    for k, r in enumerate(rows):
        gates = _sigmoid(gl_ref[r, :] + bg_ref[...])
        mix = gates[:, :D_MODEL] * ad[k] + gates[:, D_MODEL:] * sd[k]
        mixed = jnp.dot(mix.astype(BF16), wout_ref[...], preferred_element_type=F32)
        o_ref[r, :] = _layer_norm(DEEPNORM_ALPHA * x_ref[r, :] + mixed, g_ref[...], b_ref[...])


def _mix(attn, ssm, gl, x2, wpa, wps, wout, b_gates, ln_g, ln_b):
    t = x2.shape[0]
    tm = TM_MIX
    tok = lambda w: pl.BlockSpec((tm, w), lambda i: (i, 0))
    return pl.pallas_call(
        _mix_kernel,
        grid=(t // tm,),
        in_specs=[tok(ATT_WIDTH), tok(SSM_INNER), tok(2 * D_MODEL), tok(D_MODEL),
                  _const_spec(wpa.shape), _const_spec(wps.shape), _const_spec(wout.shape),
                  _const_spec((1, 2 * D_MODEL)), _const_spec((1, D_MODEL)), _const_spec((1, D_MODEL))],
        out_specs=tok(D_MODEL),
        out_shape=jax.ShapeDtypeStruct((t, D_MODEL), F32),
        compiler_params=_params("arbitrary"),
        name="mix",
    )(attn, ssm, gl, x2, wpa, wps, wout, b_gates.astype(F32)[None, :], ln_g.astype(F32)[None, :],
      ln_b.astype(F32)[None, :])


def _ffn_kernel(x_ref, wg_ref, wu_ref, wd_ref, g_ref, b_ref, o_ref):
    rows = _row_groups(x_ref.shape[0])
    xb = [x_ref[r, :].astype(BF16) for r in rows]
    hg = [jnp.dot(v, wg_ref[...], preferred_element_type=F32) for v in xb]
    hu = [jnp.dot(v, wu_ref[...], preferred_element_type=F32) for v in xb]
    for k, r in enumerate(rows):
        h = (hg[k] * _sigmoid(hg[k]) * hu[k]).astype(BF16)
        d = jnp.dot(h, wd_ref[...], preferred_element_type=F32)
        o_ref[r, :] = _layer_norm(DEEPNORM_ALPHA * x_ref[r, :] + d, g_ref[...], b_ref[...])


def _ffn(x1, wg, wu, wd, ln_g, ln_b):
    t = x1.shape[0]
    tm = TM_FFN
    tok = pl.BlockSpec((tm, D_MODEL), lambda i: (i, 0))
    return pl.pallas_call(
        _ffn_kernel,
        grid=(t // tm,),
        in_specs=[tok, _const_spec(wg.shape), _const_spec(wu.shape), _const_spec(wd.shape),
                  _const_spec((1, D_MODEL)), _const_spec((1, D_MODEL))],
        out_specs=tok,
        out_shape=jax.ShapeDtypeStruct((t, D_MODEL), F32),
        compiler_params=_params("arbitrary"),
        name="ffn",
    )(x1, wg, wu, wd, ln_g.astype(F32)[None, :], ln_b.astype(F32)[None, :])


def _pack_in_weights(w):
    o = IN_OFFS
    q, k, v, f = w[:, o[0]:o[1]], w[:, o[1]:o[2]], w[:, o[2]:o[3]], w[:, o[3]:o[4]]
    z, xbc, dt, gate = w[:, o[4]:o[5]], w[:, o[5]:o[6]], w[:, o[6]:o[7]], w[:, o[7]:o[8]]
    assert DT_OFF == 0 and F_OFF == SSM_HEADS
    bf = lambda a: a.astype(BF16)
    small = jnp.concatenate([bf(dt), bf(f), jnp.zeros((D_MODEL, SMALL_W - SSM_HEADS - ATT_HEADS), BF16)], axis=1)
    scale = LOG2E / math.sqrt(ATT_HEAD_DIM)
    return bf(q * scale), bf(k), bf(v), bf(z), bf(xbc), bf(gate), small


def kernel(x, w_in, b_forget, conv_w, conv_b, dt_bias, a_log, d_skip, ssm_norm_w, w_proj_attn,
           w_proj_ssm, b_gates, w_out, ln1_g, ln1_b, w_ffn_gate, w_ffn_up, w_ffn_down, ln2_g, ln2_b):
    bsz, seq, dm = x.shape
    assert dm == D_MODEL and seq % ATT_BLOCK == 0 and seq % SSM_CHUNK == 0
    assert w_in.shape[0] == DEPTH
    x2 = x.reshape(bsz * seq, dm)
    for l in range(DEPTH):
        k, z, xbc, gl, sm, qt, vt, smt = _inproj(x2, _pack_in_weights(w_in[l]))
        ka, xq, base = _attnprep(sm, smt, k, b_forget[l], bsz, seq)
        attn = _attention(base, qt, xq, ka, vt, bsz, seq)
        ssm = _ssd(z, xbc, sm, smt, conv_w[l], conv_b[l], dt_bias[l], a_log[l], d_skip[l], ssm_norm_w[l],
                   bsz, seq)
        x1 = _mix(attn, ssm, gl, x2, w_proj_attn[l].astype(BF16), w_proj_ssm[l].astype(BF16),
                  w_out[l].astype(BF16), b_gates[l], ln1_g[l], ln1_b[l])
        x2 = _ffn(x1, w_ffn_gate[l].astype(BF16), w_ffn_up[l].astype(BF16), w_ffn_down[l].astype(BF16),
                  ln2_g[l], ln2_b[l])
    return x2.reshape(bsz, seq, dm)
```

```python
import functools
import math

import numpy as np
import jax
import jax.numpy as jnp
from jax import lax
from jax.experimental import pallas as pl
from jax.experimental.pallas import tpu as pltpu

F32 = jnp.float32
BF16 = jnp.bfloat16

D_MODEL = 1024
ATT_HEADS = 16
ATT_HEAD_DIM = 64
ATT_WIDTH = ATT_HEADS * ATT_HEAD_DIM
SSM_INNER = 2048
SSM_HEAD_DIM = 64
SSM_HEADS = 32
SSM_GROUPS = 4
SSM_STATE = 128
SSM_CONV = 4
SSM_CHUNK = 128
SSM_CONV_DIM = SSM_INNER + 2 * SSM_GROUPS * SSM_STATE
GROUP_WIDTH = SSM_INNER // SSM_GROUPS
FFN_HIDDEN = 2816
DEPTH = 1
DEEPNORM_ALPHA = (2 * DEPTH) ** 0.25
LN_EPS = 1e-5
RMS_EPS = 1e-5
IN_SIZES = (ATT_WIDTH, ATT_WIDTH, ATT_WIDTH, ATT_HEADS, SSM_INNER, SSM_CONV_DIM, SSM_HEADS, 2 * D_MODEL)
IN_OFFS = tuple(int(v) for v in np.concatenate([[0], np.cumsum(IN_SIZES)]))

LANES = 128
VMEM_LIMIT = 56 * 1024 * 1024

TM_INPROJ = 256
WPACK_COLS = 128
ATT_BLOCK = 512
ATT_QBLOCK = 1024
TM_MIX = 512
TM_FFN = 512
ROW_GROUPS = 2
SMALL_W = 128
SMALL_T = 64
DT_OFF = 0
F_OFF = 32
XQ_ROWS = 16
ACC_ROWS = ATT_HEAD_DIM + 16
ATT_ROW_CHUNK = 64
ATT_TRIP = 4
SSD_SUB = 4
CONV_PAD = 16
LOG2E = math.log2(math.e)
NEG_BIG = -1e30


def _sigmoid(v):
    return 1.0 / (1.0 + jnp.exp(-v))


def _softplus(v):
    return jnp.maximum(v, 0.0) + jnp.log1p(jnp.exp(-jnp.abs(v)))


def _split2(v):
    hi = v.astype(BF16)
    lo = (v - hi.astype(F32)).astype(BF16)
    return hi, lo


def _split3(v):
    hi = v.astype(BF16)
    r = v - hi.astype(F32)
    lo = r.astype(BF16)
    lolo = (r - lo.astype(F32)).astype(BF16)
    return hi, lo, lolo


def _scan_rows(v):
    n = v.shape[0]
    row = lax.broadcasted_iota(jnp.int32, v.shape, 0)
    d = 1
    while d < n:
        v = v + jnp.where(row >= d, pltpu.roll(v, d, 0), 0.0)
        d *= 2
    return v


def _scan_lanes(v):
    n = v.shape[1]
    col = lax.broadcasted_iota(jnp.int32, v.shape, 1)
    d = 1
    while d < n:
        v = v + jnp.where(col >= d, pltpu.roll(v, d, 1), 0.0)
        d *= 2
    return v


def _layer_norm(y, g, b):
    mu = jnp.mean(y, axis=-1, keepdims=True)
    yc = y - mu
    var = jnp.mean(yc * yc, axis=-1, keepdims=True)
    return yc * lax.rsqrt(var + LN_EPS) * g + b


def _row_groups(n):
    g = n // ROW_GROUPS
    return [slice(k * g, (k + 1) * g) for k in range(ROW_GROUPS)]


def _const_spec(shape):
    nd = len(shape)
    return pl.BlockSpec(shape, lambda *_: (0,) * nd, pipeline_mode=pl.Buffered(1))


def _params(*sem):
    return pltpu.CompilerParams(dimension_semantics=sem, vmem_limit_bytes=VMEM_LIMIT)


def _inproj_kernel(x_ref, wq_ref, wk_ref, wv_ref, wz_ref, wxbc_ref, wg_ref, ws_ref,
                   k_ref, z_ref, xbc_ref, gl_ref, sm_ref, qt_ref, vt_ref, smt_ref):
    xb = x_ref[...].astype(BF16)

    def proj(w_ref):
        return jnp.dot(xb, w_ref[...], preferred_element_type=F32)

    k_ref[...] = proj(wk_ref).astype(BF16)
    z_ref[...] = proj(wz_ref)
    xbc_ref[...] = proj(wxbc_ref).astype(BF16)
    gl_ref[...] = proj(wg_ref)
    sm = proj(ws_ref)
    sm_ref[...] = sm
    qt_ref[...] = proj(wq_ref).T.astype(BF16)
    vt_ref[...] = proj(wv_ref).T.astype(BF16)
    smt_ref[...] = sm.T[0:SMALL_T, :]


def _inproj(x2, weights):
    t = x2.shape[0]
    tm = TM_INPROJ
    tok = lambda w: pl.BlockSpec((tm, w), lambda i: (i, 0))
    feat = lambda h: pl.BlockSpec((h, tm), lambda i: (0, i))
    return pl.pallas_call(
        _inproj_kernel,
        grid=(t // tm,),
        in_specs=[tok(D_MODEL)] + [_const_spec(w.shape) for w in weights],
        out_specs=[tok(1024), tok(2048), tok(3072), tok(2048), tok(SMALL_W),
                   feat(1024), feat(1024), feat(SMALL_T)],
        out_shape=[
            jax.ShapeDtypeStruct((t, 1024), BF16),
            jax.ShapeDtypeStruct((t, 2048), F32),
            jax.ShapeDtypeStruct((t, 3072), BF16),
            jax.ShapeDtypeStruct((t, 2048), F32),
            jax.ShapeDtypeStruct((t, SMALL_W), F32),
            jax.ShapeDtypeStruct((1024, t), BF16),
            jax.ShapeDtypeStruct((1024, t), BF16),
            jax.ShapeDtypeStruct((SMALL_T, t), F32),
        ],
        compiler_params=_params("arbitrary"),
        name="inproj",
    )(x2, *weights)


def _attnprep_kernel(sm_ref, smt_ref, k_ref, bfrow_ref, bfcol_ref, shi_ref, crow_ref,
                     phi_ref, plo_ref, pll_ref, ka_ref, xq_ref, base_ref, carry_ref):
    i = pl.program_id(1)

    @pl.when(i == 0)
    def _():
        carry_ref[...] = jnp.zeros_like(carry_ref)

    g = sm_ref.shape[0]
    v = sm_ref[...] + bfrow_ref[...]
    lf = -_softplus(-v) * LOG2E
    rk = _scan_rows(lf)
    base_ref[0] = carry_ref[...]
    carry_ref[...] = carry_ref[...] + rk[g - 1:g, :]
    lane = lax.broadcasted_iota(jnp.int32, (g, LANES), 1)
    low = lane < ATT_HEAD_DIM
    heads = (lane >= F_OFF) & (lane < F_OFF + ATT_HEADS)
    hi, lo, ll = (jnp.where(heads, part.astype(F32), 0.0) for part in _split3(rk))
    packed = hi + pltpu.roll(lo, ATT_HEADS, 1) + pltpu.roll(ll, 2 * ATT_HEADS, 1)
    extras = jnp.dot(packed.astype(BF16), shi_ref[...], preferred_element_type=F32) + crow_ref[...]
    for j in range(ATT_HEADS // 2):
        kk = k_ref[:, LANES * j:LANES * (j + 1)].astype(F32)
        ev = extras[:, 2 * LANES * j:2 * LANES * j + LANES]
        od = extras[:, 2 * LANES * j + LANES:2 * LANES * (j + 1)]
        ka_ref[:, 2 * LANES * j:2 * LANES * j + LANES] = jnp.where(low, kk, ev).astype(BF16)
        ka_ref[:, 2 * LANES * j + LANES:2 * LANES * (j + 1)] = jnp.where(low, od, kk).astype(BF16)

    vt = smt_ref[...] + bfcol_ref[...]
    lft = -_softplus(-vt) * LOG2E
    rq = _scan_lanes(lft)
    hq, lq, llq = _split3(rq)
    nrow = ATT_HEADS * XQ_ROWS
    row = lax.broadcasted_iota(jnp.int32, (nrow, g), 0)
    ones = jnp.where((row & (XQ_ROWS - 1)) < 3, 1.0, 0.0)
    xq = (jnp.dot(phi_ref[...], hq, preferred_element_type=F32)
          + jnp.dot(plo_ref[...], lq, preferred_element_type=F32)
          + jnp.dot(pll_ref[...], llq, preferred_element_type=F32)
          + ones)
    xq_ref[...] = xq.astype(BF16)


def _attnprep_consts():
    assert F_OFF + 3 * ATT_HEADS <= SMALL_W
    s_hi = np.zeros((SMALL_W, ATT_HEADS * LANES), np.float32)
    crow = np.zeros((1, ATT_HEADS * LANES), np.float32)
    p_hi = np.zeros((ATT_HEADS * XQ_ROWS, SMALL_T), np.float32)
    p_lo = np.zeros_like(p_hi)
    p_ll = np.zeros_like(p_hi)
    for h in range(ATT_HEADS):
        off = LANES * h + (ATT_HEAD_DIM if h % 2 == 0 else 0)
        for part in range(3):
            s_hi[F_OFF + part * ATT_HEADS + h, off + part] = -1.0
        crow[0, off + 3:off + 6] = 1.0
        p_hi[XQ_ROWS * h + 3, F_OFF + h] = 1.0
        p_lo[XQ_ROWS * h + 4, F_OFF + h] = 1.0
        p_ll[XQ_ROWS * h + 5, F_OFF + h] = 1.0
    b = lambda a: jnp.asarray(a, BF16)
    return b(s_hi), jnp.asarray(crow), b(p_hi), b(p_lo), b(p_ll)


def _attnprep(sm, smt, k, b_forget, bsz, seq):
    g = ATT_BLOCK
    nb = seq // g
    s_hi, crow, p_hi, p_lo, p_ll = _attnprep_consts()
    bfrow = jnp.zeros((1, SMALL_W), F32).at[0, F_OFF:F_OFF + ATT_HEADS].set(b_forget.astype(F32))
    bfcol = jnp.zeros((SMALL_T, 1), F32).at[F_OFF:F_OFF + ATT_HEADS, 0].set(b_forget.astype(F32))
    bfcol = jnp.broadcast_to(bfcol, (SMALL_T, g))
    t = bsz * seq
    ka, xq, base = pl.pallas_call(
        _attnprep_kernel,
        grid=(bsz, nb),
        in_specs=[
            pl.BlockSpec((g, SMALL_W), lambda b, i: (b * nb + i, 0)),
            pl.BlockSpec((SMALL_T, g), lambda b, i: (0, b * nb + i)),
            pl.BlockSpec((g, ATT_WIDTH), lambda b, i: (b * nb + i, 0)),
            _const_spec(bfrow.shape), _const_spec(bfcol.shape),
            _const_spec(s_hi.shape), _const_spec(crow.shape),
            _const_spec(p_hi.shape), _const_spec(p_lo.shape), _const_spec(p_ll.shape),
        ],
        out_specs=[
            pl.BlockSpec((g, ATT_HEADS * LANES), lambda b, i: (b * nb + i, 0)),
            pl.BlockSpec((ATT_HEADS * XQ_ROWS, g), lambda b, i: (0, b * nb + i)),
            pl.BlockSpec((1, 1, SMALL_W), lambda b, i: (b * nb + i, 0, 0)),
        ],
        out_shape=[
            jax.ShapeDtypeStruct((t, ATT_HEADS * LANES), BF16),
            jax.ShapeDtypeStruct((ATT_HEADS * XQ_ROWS, t), BF16),
            jax.ShapeDtypeStruct((bsz * nb, 1, SMALL_W), F32),
        ],
        scratch_shapes=[pltpu.VMEM((1, SMALL_W), F32)],
        compiler_params=_params("arbitrary", "arbitrary"),
        name="attnprep",
    )(sm, smt, k, bfrow, bfcol, s_hi, crow, p_hi, p_lo, p_ll)
    base = base.reshape(bsz, nb, SMALL_W)[:, :, F_OFF:F_OFF + ATT_HEADS]
    base = jnp.transpose(base, (0, 2, 1)).reshape(-1)
    return ka, xq, base


def _attn_kernel(base_ref, qt_ref, xq_ref, ka_ref, vt_ref, o_ref, *scratch, nb):
    b = pl.program_id(0)
    pr = pl.program_id(1)
    i = pl.program_id(2)
    tq = qt_ref.shape[1]
    tk = ATT_BLOCK
    dh = ATT_HEAD_DIM
    s_slots = tuple(scratch[2 * d:2 * d + 2] for d in range(ATT_TRIP))
    p_refs = scratch[2 * ATT_TRIP:2 * ATT_TRIP + 2]
    acc_refs = scratch[2 * ATT_TRIP + 2:2 * ATT_TRIP + 4]
    m_refs = scratch[2 * ATT_TRIP + 4:2 * ATT_TRIP + 6]
    zpad = jnp.zeros((LANES - dh - XQ_ROWS, tq), BF16)
    qa = (jnp.concatenate([qt_ref[0:dh, :], xq_ref[0:XQ_ROWS, :], zpad], axis=0),
          jnp.concatenate([xq_ref[XQ_ROWS:2 * XQ_ROWS, :], zpad, qt_ref[dh:2 * dh, :]], axis=0))
    boff = [(b * ATT_HEADS + 2 * pr + e) * nb for e in range(2)]
    ones_rows = jnp.ones((ACC_ROWS - dh, tk), BF16)
    for e in range(2):
        acc_refs[e][...] = jnp.zeros_like(acc_refs[e])
        m_refs[e][...] = jnp.full(m_refs[e].shape, NEG_BIG, F32)
    qlane = lax.broadcasted_iota(jnp.int32, (1, tq), 1)
    qbase = [jnp.where(qlane >= tk, base_ref[boff[e] + 2 * i + 1], base_ref[boff[e] + 2 * i]) for e in range(2)]

    def scores(j, s_ref, e, lo):
        off = pl.multiple_of(j * tk, tk)
        kj = ka_ref[pl.ds(off, tk), LANES * e:LANES * (e + 1)]
        s = jnp.dot(kj, qa[e][:, lo:], preferred_element_type=F32)
        s_ref[:, lo:] = s
        return jnp.max(s, axis=0, keepdims=True)

    def accumulate(j, s_ref, mt, m, e, lo, masked):
        off = pl.multiple_of(j * tk, tk)
        w = tq - lo
        c = qbase[e][:, lo:] - base_ref[boff[e] + j]
        p_ref, acc_ref = p_refs[e], acc_refs[e]
        if masked:
            mt = None
            for r in range(tk // ATT_ROW_CHUNK):
                rows = slice(r * ATT_ROW_CHUNK, (r + 1) * ATT_ROW_CHUNK)
                krow = lax.broadcasted_iota(jnp.int32, (ATT_ROW_CHUNK, w), 0) + r * ATT_ROW_CHUNK
                qcol = lax.broadcasted_iota(jnp.int32, (ATT_ROW_CHUNK, w), 1)
                sc = jnp.where(krow <= qcol, s_ref[rows, lo:], NEG_BIG)
                s_ref[rows, lo:] = sc
                cm = jnp.max(sc, axis=0, keepdims=True)
                mt = cm if mt is None else jnp.maximum(mt, cm)
        m_old = m[:, lo:]
        m_new = jnp.maximum(m_old, mt + c)
        alpha = jnp.exp2(m_old - m_new)
        sh = m_new - c
        for r in range(tk // ATT_ROW_CHUNK):
            rows = slice(r * ATT_ROW_CHUNK, (r + 1) * ATT_ROW_CHUNK)
            p_ref[rows, lo:] = jnp.exp2(s_ref[rows, lo:] - sh).astype(BF16)
        vj = vt_ref[dh * e:dh * (e + 1), pl.ds(off, tk)]
        vaug = jnp.concatenate([vj, ones_rows], axis=0)
        acc_ref[:, lo:] = alpha * acc_ref[:, lo:] + jnp.dot(vaug, p_ref[:, lo:], preferred_element_type=F32)
        return m_new if lo == 0 else jnp.concatenate([m[:, :lo], m_new], axis=1)

    def trip(blocks):
        los = [0 if d is None else d * tk for _, d in blocks]
        mts = [[scores(j, s_slots[k][e], e, los[k]) for e in range(2)] for k, (j, _) in enumerate(blocks)]
        ms = [m_refs[e][0:1, :] for e in range(2)]
        for k, (j, d) in enumerate(blocks):
            ms = [accumulate(j, s_slots[k][e], mts[k][e], ms[e], e, los[k], d is not None) for e in range(2)]
        for e in range(2):
            m_refs[e][0:1, :] = ms[e]

    def full_trip(t, carry):
        trip([(ATT_TRIP * t + k, None) for k in range(ATT_TRIP)])
        return carry

    assert ATT_TRIP == 4
    lax.fori_loop(0, (2 * i) // ATT_TRIP, full_trip, 0)
    diag_blocks = [(2 * i, 0), (2 * i + 1, 1)]

    @pl.when(i % 2 == 0)
    def _():
        trip(diag_blocks)

    @pl.when(i % 2 == 1)
    def _():
        trip([(2 * i - 2, None), (2 * i - 1, None)] + diag_blocks)

    ot = jnp.concatenate([a_ref[0:dh, :] / a_ref[dh:dh + 1, :] for a_ref in acc_refs], axis=0)
    o_ref[...] = ot.T.astype(o_ref.dtype)


def _attention(base, qt, xq, ka, vt, bsz, seq):
    tq = ATT_QBLOCK
    tk = ATT_BLOCK
    assert tq == 2 * tk and seq % tq == 0
    nq = seq // tq
    nb = seq // tk
    npair = ATT_HEADS // 2
    t = bsz * seq
    return pl.pallas_call(
        functools.partial(_attn_kernel, nb=nb),
        grid=(bsz, npair, nq),
        in_specs=[
            pl.BlockSpec(memory_space=pltpu.SMEM),
            pl.BlockSpec((2 * ATT_HEAD_DIM, tq), lambda b, p, i: (p, b * nq + i)),
            pl.BlockSpec((2 * XQ_ROWS, tq), lambda b, p, i: (p, b * nq + i)),
            pl.BlockSpec((seq, 2 * LANES), lambda b, p, i: (b, p)),
            pl.BlockSpec((2 * ATT_HEAD_DIM, seq), lambda b, p, i: (p, b)),
        ],
        out_specs=pl.BlockSpec((tq, 2 * ATT_HEAD_DIM), lambda b, p, i: (b * nq + i, p)),
        out_shape=jax.ShapeDtypeStruct((t, ATT_WIDTH), BF16),
        scratch_shapes=([pltpu.VMEM((tk, tq), F32)] * (2 * ATT_TRIP)
                        + [pltpu.VMEM((tk, tq), BF16)] * 2
                        + [pltpu.VMEM((ACC_ROWS, tq), F32)] * 2
                        + [pltpu.VMEM((8, tq), F32)] * 2),
        compiler_params=_params("arbitrary", "arbitrary", "arbitrary"),
        name="attn",
    )(base, qt, xq, ka, vt)


def _ssd_kernel(z_ref, xbc_ref, sm_ref, smt_ref, cw_ref, cwb_ref, cb_ref, shift_ref, dtbrow_ref, dtbcol_ref,
                arow_ref, acol_ref, dexp_ref, nw_ref, e2_ref, o_ref, prev_ref, state_ref):
    c = pl.program_id(1)
    L = SSM_CHUNK

    @pl.when(c == 0)
    def _():
        prev_ref[0:CONV_PAD, :] = jnp.zeros((CONV_PAD, SSM_CONV_DIM), BF16)
        state_ref[...] = jnp.zeros_like(state_ref)

    prev_ref[CONV_PAD:, :] = xbc_ref[...]
    for sub in range(SSD_SUB):
        _ssd_chunk(slice(sub * L, (sub + 1) * L), z_ref, xbc_ref, sm_ref, smt_ref, cw_ref, cwb_ref, cb_ref,
                   shift_ref, dtbrow_ref, dtbcol_ref, arow_ref, acol_ref, dexp_ref, nw_ref, e2_ref, o_ref,
                   prev_ref, state_ref)
    prev_ref[0:CONV_PAD, :] = xbc_ref[SSD_SUB * L - CONV_PAD:SSD_SUB * L, :]


def _ssd_chunk(rows, z_ref, xbc_ref, sm_ref, smt_ref, cw_ref, cwb_ref, cb_ref, shift_ref, dtbrow_ref, dtbcol_ref,
               arow_ref, acol_ref, dexp_ref, nw_ref, e2_ref, o_ref, prev_ref, state_ref):
    L = SSM_CHUNK
    N = SSM_STATE
    GW = GROUP_WIDTH
    ub = xbc_ref[rows, :]
    win = prev_ref[rows.start:rows.start + CONV_PAD + L, :]
    taps = jnp.concatenate([win * cwb_ref[SSM_CONV - 1 - s:SSM_CONV - s, :] for s in range(1, SSM_CONV)],
                           axis=0)
    acc = (ub.astype(F32) * cw_ref[SSM_CONV - 1:SSM_CONV, :] + cb_ref[...]
           + jnp.dot(shift_ref[...], taps, preferred_element_type=F32))
    xc = acc * _sigmoid(acc)
    xs = xc[:, :SSM_INNER]
    bm = xc[:, SSM_INNER:SSM_INNER + SSM_GROUPS * N].astype(BF16)
    cm = xc[:, SSM_INNER + SSM_GROUPS * N:].astype(BF16)

    a_row = -jnp.exp(arow_ref[...])
    dt = _softplus(sm_ref[rows, :] + dtbrow_ref[...])
    acum = _scan_rows(dt * a_row)
    a_col = -jnp.exp(acol_ref[...])
    dtt = _softplus(smt_ref[:, rows] + dtbcol_ref[...])
    acumt = _scan_lanes(dtt * a_col)
    acum2, acumt2 = acum * LOG2E, acumt * LOG2E
    alast = acum[L - 1:L, :]
    ea = jnp.exp(acum)
    w2 = dt * jnp.exp(alast - acum)

    def expand(v):
        hi, lo = _split2(v)
        return jnp.dot(jnp.concatenate([hi, lo], axis=1), e2_ref[...], preferred_element_type=F32)

    dt_e = expand(dt)
    ea_e = expand(ea)
    w2_e = expand(w2)
    cd_e = ea_e[L - 1:L, :]

    xdt = (xs * dt_e).astype(BF16)
    xw2 = (xs * w2_e).astype(BF16)

    r2 = lax.broadcasted_iota(jnp.int32, (L, L), 0)
    c2 = lax.broadcasted_iota(jnp.int32, (L, L), 1)
    tri = r2 >= c2
    lane = lax.broadcasted_iota(jnp.int32, (L, LANES), 1)
    low = lane < SSM_HEAD_DIM

    ys = []
    for g in range(SSM_GROUPS):
        bg = bm[:, N * g:N * (g + 1)]
        cg = cm[:, N * g:N * (g + 1)]
        cb = lax.dot_general(cg, bg, (((1,), (1,)), ((), ())), preferred_element_type=F32)
        st_prev = state_ref[g]
        y_off = jnp.dot(cg, st_prev.astype(BF16), preferred_element_type=F32) * ea_e[:, GW * g:GW * (g + 1)]
        st_c = lax.dot_general(bg, xw2[:, GW * g:GW * (g + 1)], (((0,), (0,)), ((), ())),
                               preferred_element_type=F32)
        state_ref[g] = st_prev * cd_e[:, GW * g:GW * (g + 1)] + st_c
        yd = []
        for pj in range(GW // LANES):
            h0 = (GW // SSM_HEAD_DIM) * g + 2 * pj
            ms = []
            for h in (h0, h0 + 1):
                diff = acum2[:, h:h + 1] - acumt2[h:h + 1, :]
                ms.append((cb * jnp.exp2(jnp.where(tri, diff, NEG_BIG))).astype(BF16))
            xp = xdt[:, LANES * (h0 // 2):LANES * (h0 // 2 + 1)]
            zero = jnp.zeros_like(xp)
            rhs = jnp.concatenate([jnp.where(low, xp, zero), jnp.where(low, zero, xp)], axis=0)
            yd.append(jnp.dot(jnp.concatenate(ms, axis=1), rhs, preferred_element_type=F32))
        ys.append(jnp.concatenate(yd, axis=1) + y_off)
    y = jnp.concatenate(ys, axis=1) + dexp_ref[...] * xs

    zz = z_ref[rows, :]
    uu = y * (zz * _sigmoid(zz))
    outs = []
    for g in range(SSM_GROUPS):
        ug = uu[:, GW * g:GW * (g + 1)]
        ms_ = jnp.mean(ug * ug, axis=-1, keepdims=True)
        outs.append(ug * lax.rsqrt(ms_ + RMS_EPS))
    o_ref[rows, :] = (jnp.concatenate(outs, axis=1) * nw_ref[...]).astype(o_ref.dtype)


def _ssd(z, xbc, sm, smt, conv_w, conv_b, dt_bias, a_log, d_skip, norm_w, bsz, seq):
    L = SSM_CHUNK
    rows = SSD_SUB * L
    assert seq % rows == 0
    nc = seq // rows
    t = bsz * seq
    dtbrow = jnp.zeros((1, SMALL_W), F32).at[0, DT_OFF:DT_OFF + SSM_HEADS].set(dt_bias.astype(F32))
    arow = jnp.zeros((1, SMALL_W), F32).at[0, DT_OFF:DT_OFF + SSM_HEADS].set(a_log.astype(F32))
    dtbcol = jnp.zeros((SMALL_T, 1), F32).at[DT_OFF:DT_OFF + SSM_HEADS, 0].set(dt_bias.astype(F32))
    acol = jnp.zeros((SMALL_T, 1), F32).at[DT_OFF:DT_OFF + SSM_HEADS, 0].set(a_log.astype(F32))
    dtbcol = jnp.broadcast_to(dtbcol, (SMALL_T, L))
    acol = jnp.broadcast_to(acol, (SMALL_T, L))
    dexp = jnp.repeat(d_skip.astype(F32), SSM_HEAD_DIM)[None, :]
    e1 = np.zeros((SMALL_W, SSM_INNER), np.float32)
    for h in range(SSM_HEADS):
        e1[DT_OFF + h, SSM_HEAD_DIM * h:SSM_HEAD_DIM * (h + 1)] = 1.0
    e2 = jnp.asarray(np.concatenate([e1, e1], axis=0), BF16)
    win = CONV_PAD + L
    sh = np.zeros((L, (SSM_CONV - 1) * win), np.float32)
    for s in range(1, SSM_CONV):
        sh[np.arange(L), (s - 1) * win + CONV_PAD + np.arange(L) - s] = 1.0
    shift = jnp.asarray(sh, BF16)
    tok = lambda w: pl.BlockSpec((rows, w), lambda b, c: (b * nc + c, 0))
    return pl.pallas_call(
        _ssd_kernel,
        grid=(bsz, nc),
        in_specs=[
            tok(SSM_INNER), tok(SSM_CONV_DIM), tok(SMALL_W),
            pl.BlockSpec((SMALL_T, rows), lambda b, c: (0, b * nc + c)),
            _const_spec((SSM_CONV, SSM_CONV_DIM)), _const_spec((SSM_CONV, SSM_CONV_DIM)),
            _const_spec((1, SSM_CONV_DIM)), _const_spec(shift.shape),
            _const_spec(dtbrow.shape), _const_spec(dtbcol.shape), _const_spec(arow.shape), _const_spec(acol.shape),
            _const_spec(dexp.shape), _const_spec((1, SSM_INNER)), _const_spec(e2.shape),
        ],
        out_specs=tok(SSM_INNER),
        out_shape=jax.ShapeDtypeStruct((t, SSM_INNER), BF16),
        scratch_shapes=[pltpu.VMEM((CONV_PAD + rows, SSM_CONV_DIM), BF16),
                        pltpu.VMEM((SSM_GROUPS, SSM_STATE, GROUP_WIDTH), F32)],
        compiler_params=_params("arbitrary", "arbitrary"),
        name="ssd",
    )(z, xbc, sm, smt, conv_w.astype(F32), conv_w.astype(BF16), conv_b.astype(F32)[None, :], shift,
      dtbrow, dtbcol, arow, acol, dexp, norm_w.astype(F32)[None, :], e2)


def _mix_kernel(attn_ref, ssm_ref, gl_ref, x_ref, wpa_ref, wps_ref, wout_ref, bg_ref, g_ref, b_ref, o_ref):
    rows = _row_groups(x_ref.shape[0])
    ad = [jnp.dot(attn_ref[r, :], wpa_ref[...], preferred_element_type=F32) for r in rows]
    sd = [jnp.dot(ssm_ref[r, :], wps_ref[...], preferred_element_type=F32) for r in rows]
    for k, r in enumerate(rows):
        gates = _sigmoid(gl_ref[r, :] + bg_ref[...])
        mix = gates[:, :D_MODEL] * ad[k] + gates[:, D_MODEL:] * sd[k]
        mixed = jnp.dot(mix.astype(BF16), wout_ref[...], preferred_element_type=F32)
        o_ref[r, :] = _layer_norm(DEEPNORM_ALPHA * x_ref[r, :] + mixed, g_ref[...], b_ref[...])


def _mix(attn, ssm, gl, x2, wpa, wps, wout, b_gates, ln_g, ln_b):
    t = x2.shape[0]
    tm = TM_MIX
    tok = lambda w: pl.BlockSpec((tm, w), lambda i: (i, 0))
    return pl.pallas_call(
        _mix_kernel,
        grid=(t // tm,),
        in_specs=[tok(ATT_WIDTH), tok(SSM_INNER), tok(2 * D_MODEL), tok(D_MODEL),
                  _const_spec(wpa.shape), _const_spec(wps.shape), _const_spec(wout.shape),
                  _const_spec((1, 2 * D_MODEL)), _const_spec((1, D_MODEL)), _const_spec((1, D_MODEL))],
        out_specs=tok(D_MODEL),
        out_shape=jax.ShapeDtypeStruct((t, D_MODEL), F32),
        compiler_params=_params("arbitrary"),
        name="mix",
    )(attn, ssm, gl, x2, wpa, wps, wout, b_gates.astype(F32)[None, :], ln_g.astype(F32)[None, :],
      ln_b.astype(F32)[None, :])


def _ffn_kernel(x_ref, wg_ref, wu_ref, wd_ref, g_ref, b_ref, o_ref):
    rows = _row_groups(x_ref.shape[0])
    xb = [x_ref[r, :].astype(BF16) for r in rows]
    hg = [jnp.dot(v, wg_ref[...], preferred_element_type=F32) for v in xb]
    hu = [jnp.dot(v, wu_ref[...], preferred_element_type=F32) for v in xb]
    for k, r in enumerate(rows):
        h = (hg[k] * _sigmoid(hg[k]) * hu[k]).astype(BF16)
        d = jnp.dot(h, wd_ref[...], preferred_element_type=F32)
        o_ref[r, :] = _layer_norm(DEEPNORM_ALPHA * x_ref[r, :] + d, g_ref[...], b_ref[...])


def _ffn(x1, wg, wu, wd, ln_g, ln_b):
    t = x1.shape[0]
    tm = TM_FFN
    tok = pl.BlockSpec((tm, D_MODEL), lambda i: (i, 0))
    return pl.pallas_call(
        _ffn_kernel,
        grid=(t // tm,),
        in_specs=[tok, _const_spec(wg.shape), _const_spec(wu.shape), _const_spec(wd.shape),
                  _const_spec((1, D_MODEL)), _const_spec((1, D_MODEL))],
        out_specs=tok,
        out_shape=jax.ShapeDtypeStruct((t, D_MODEL), F32),
        compiler_params=_params("arbitrary"),
        name="ffn",
    )(x1, wg, wu, wd, ln_g.astype(F32)[None, :], ln_b.astype(F32)[None, :])


def _wpack_kernel(wt_ref, q_ref, k_ref, v_ref, z_ref, xbc_ref, g_ref, s_ref):
    o = IN_OFFS
    scale = LOG2E / math.sqrt(ATT_HEAD_DIM)
    q_ref[...] = (wt_ref[o[0]:o[1], :] * scale).T.astype(BF16)
    k_ref[...] = wt_ref[o[1]:o[2], :].T.astype(BF16)
    v_ref[...] = wt_ref[o[2]:o[3], :].T.astype(BF16)
    z_ref[...] = wt_ref[o[4]:o[5], :].T.astype(BF16)
    xbc_ref[...] = wt_ref[o[5]:o[6], :].T.astype(BF16)
    g_ref[...] = wt_ref[o[7]:o[8], :].T.astype(BF16)
    assert DT_OFF == 0 and F_OFF == SSM_HEADS
    pad = jnp.zeros((SMALL_W - SSM_HEADS - ATT_HEADS, wt_ref.shape[1]), F32)
    small = jnp.concatenate([wt_ref[o[6]:o[7], :], wt_ref[o[3]:o[4], :], pad], axis=0)
    s_ref[...] = small.T.astype(BF16)


def _pack_in_weights(w):
    wt = w.T
    cols = WPACK_COLS
    widths = (ATT_WIDTH, ATT_WIDTH, ATT_WIDTH, SSM_INNER, SSM_CONV_DIM, 2 * D_MODEL, SMALL_W)
    return pl.pallas_call(
        _wpack_kernel,
        grid=(D_MODEL // cols,),
        in_specs=[pl.BlockSpec((wt.shape[0], cols), lambda i: (0, i))],
        out_specs=[pl.BlockSpec((cols, n), lambda i: (i, 0)) for n in widths],
        out_shape=[jax.ShapeDtypeStruct((D_MODEL, n), BF16) for n in widths],
        compiler_params=_params("arbitrary"),
        name="wpack",
    )(wt)


def kernel(x, w_in, b_forget, conv_w, conv_b, dt_bias, a_log, d_skip, ssm_norm_w, w_proj_attn,
           w_proj_ssm, b_gates, w_out, ln1_g, ln1_b, w_ffn_gate, w_ffn_up, w_ffn_down, ln2_g, ln2_b):
    bsz, seq, dm = x.shape
    assert dm == D_MODEL and seq % ATT_BLOCK == 0 and seq % SSM_CHUNK == 0
    assert w_in.shape[0] == DEPTH
    x2 = x.reshape(bsz * seq, dm)
    for l in range(DEPTH):
        k, z, xbc, gl, sm, qt, vt, smt = _inproj(x2, _pack_in_weights(w_in[l]))
        ka, xq, base = _attnprep(sm, smt, k, b_forget[l], bsz, seq)
        attn = _attention(base, qt, xq, ka, vt, bsz, seq)
        ssm = _ssd(z, xbc, sm, smt, conv_w[l], conv_b[l], dt_bias[l], a_log[l], d_skip[l], ssm_norm_w[l],
                   bsz, seq)
        x1 = _mix(attn, ssm, gl, x2, w_proj_attn[l].astype(BF16), w_proj_ssm[l].astype(BF16),
                  w_out[l].astype(BF16), b_gates[l], ln1_g[l], ln1_b[l])
        x2 = _ffn(x1, w_ffn_gate[l].astype(BF16), w_ffn_up[l].astype(BF16), w_ffn_down[l].astype(BF16),
                  ln2_g[l], ln2_b[l])
    return x2.reshape(bsz, seq, dm)
```
